```python
import math
import jax, jax.numpy as jnp
from jax import lax
import numpy as np


D_MODEL = 4096
BATCH = 4
SEQ = 2048
DEPTH = 1

ATTN_WIDTH = D_MODEL // 2
CONV_WIDTH = D_MODEL - ATTN_WIDTH
MIX_WIDTH = ATTN_WIDTH + CONV_WIDTH
HEAD_DIM = 128
V_HEAD_DIM = 2 * HEAD_DIM
N_DIFF_HEADS = ATTN_WIDTH // V_HEAD_DIM
CONV_GROUP = 128
N_CONV_GROUPS = CONV_WIDTH // CONV_GROUP
CONV_TAPS = 31
ROPE_THETA = 500000.0
ROPE_DIM = HEAD_DIM // 4
D_FF = -(-8 * D_MODEL // (3 * 256)) * 256
Q_BLOCK = 128
Q_COLS = N_DIFF_HEADS * 2 * HEAD_DIM
K_COLS = N_DIFF_HEADS * 2 * HEAD_DIM
V_COLS = N_DIFF_HEADS * V_HEAD_DIM
GLU_COLS = 2 * CONV_WIDTH
N_IN = Q_COLS + K_COLS + V_COLS + GLU_COLS
DEEPNORM_ALPHA = (2 * DEPTH) ** 0.25
DEEPNORM_BETA = (8 * DEPTH) ** -0.25
LN_EPS = 1e-5
MAX_POS_OFFSET = 1024

kernel_name = 'hybrid_diffattn_conformer_deepnorm'


def layer_norm(x, g, b):
    xf = x.astype(jnp.float32)
    mu = jnp.mean(xf, axis=-1, keepdims=True)
    xc = xf - mu
    var = jnp.mean(xc * xc, axis=-1, keepdims=True)
    y = xc * lax.rsqrt(var + LN_EPS)
    return (y * g.astype(jnp.float32) + b.astype(jnp.float32)).astype(x.dtype)


def rms_norm(x, g):
    xf = x.astype(jnp.float32)
    y = xf * lax.rsqrt(jnp.mean(xf * xf, axis=-1, keepdims=True) + LN_EPS)
    return (y * g.astype(jnp.float32)).astype(x.dtype)


def partial_rotary(t, cos, sin):
    half = ROPE_DIM // 2
    t1 = t[..., :half]
    t2 = t[..., half:ROPE_DIM]
    rot = jnp.concatenate([t1 * cos - t2 * sin, t2 * cos + t1 * sin], axis=-1)
    return jnp.concatenate([rot, t[..., ROPE_DIM:]], axis=-1)


def diff_attention(q1, q2, k1, k2, v, lam):
    B, H, S, _ = q1.shape
    nb = S // Q_BLOCK
    scale = HEAD_DIM ** -0.5
    qb1 = jnp.moveaxis(q1.reshape(B, H, nb, Q_BLOCK, HEAD_DIM), 2, 0)
    qb2 = jnp.moveaxis(q2.reshape(B, H, nb, Q_BLOCK, HEAD_DIM), 2, 0)
    kpos = jnp.arange(S)
    vf = v.astype(jnp.float32)

    def block(args):
        i, qa, qb = args
        qpos = i * Q_BLOCK + jnp.arange(Q_BLOCK)
        mask = kpos[None, :] <= qpos[:, None]
        s1 = jnp.einsum('bhqd,bhkd->bhqk', qa, k1).astype(jnp.float32) * scale
        s2 = jnp.einsum('bhqd,bhkd->bhqk', qb, k2).astype(jnp.float32) * scale
        p1 = jax.nn.softmax(jnp.where(mask, s1, -jnp.inf), axis=-1)
        p2 = jax.nn.softmax(jnp.where(mask, s2, -jnp.inf), axis=-1)
        return jnp.einsum('bhqk,bhkd->bhqd', p1 - lam * p2, vf)

    out = lax.map(block, (jnp.arange(nb), qb1, qb2))
    out = jnp.moveaxis(out, 0, 2).reshape(B, H, S, V_HEAD_DIM)
    return jnp.transpose(out, (0, 2, 1, 3)).astype(v.dtype)


def hybrid_mixer(x, cos, sin, w_in, b_glu, conv_w, conv_b, conv_ln_g, conv_ln_b,
                 lam_q1, lam_k1, lam_q2, lam_k2, subln_g, w_o, lambda_init):
    B, S, _ = x.shape
    proj = jnp.einsum('bsd,dn->bsn', x, w_in)
    q = proj[..., :Q_COLS].reshape(B, S, N_DIFF_HEADS, 2, HEAD_DIM)
    k = proj[..., Q_COLS:Q_COLS + K_COLS].reshape(B, S, N_DIFF_HEADS, 2, HEAD_DIM)
    v = proj[..., Q_COLS + K_COLS:Q_COLS + K_COLS + V_COLS].reshape(B, S, N_DIFF_HEADS, V_HEAD_DIM)
    u = proj[..., Q_COLS + K_COLS + V_COLS:]

    c5 = cos[:, :, None, None, :]
    s5 = sin[:, :, None, None, :]
    q = partial_rotary(q, c5, s5)
    k = partial_rotary(k, c5, s5)
    q1 = jnp.transpose(q[..., 0, :], (0, 2, 1, 3))
    q2 = jnp.transpose(q[..., 1, :], (0, 2, 1, 3))
    k1 = jnp.transpose(k[..., 0, :], (0, 2, 1, 3))
    k2 = jnp.transpose(k[..., 1, :], (0, 2, 1, 3))
    vh = jnp.transpose(v, (0, 2, 1, 3))
    lam = (jnp.exp(jnp.sum(lam_q1.astype(jnp.float32) * lam_k1.astype(jnp.float32)))
           - jnp.exp(jnp.sum(lam_q2.astype(jnp.float32) * lam_k2.astype(jnp.float32)))
           + lambda_init)
    o = diff_attention(q1, q2, k1, k2, vh, lam)
    o = rms_norm(o, subln_g) * (1.0 - lambda_init)
    attn_out = o.reshape(B, S, ATTN_WIDTH)

    u = u + b_glu
    a, gate = u[..., :CONV_WIDTH], u[..., CONV_WIDTH:]
    c = a * jax.nn.sigmoid(gate)
    cp = jnp.pad(c, ((0, 0), (CONV_TAPS - 1, 0), (0, 0)))
    c = lax.conv_general_dilated(cp, conv_w[:, None, :], window_strides=(1,), padding='VALID',
                                 dimension_numbers=('NWC', 'WIO', 'NWC'),
                                 feature_group_count=CONV_WIDTH) + conv_b
    c = jax.nn.silu(layer_norm(c, conv_ln_g, conv_ln_b))

    mix = jnp.concatenate([attn_out, c], axis=-1)
    return jnp.einsum('bsm,md->bsd', mix, w_o)


def swiglu_ffn(h, w_gate, w_up, w_down):
    hid = jax.nn.silu(jnp.einsum('bsd,df->bsf', h, w_gate)) * jnp.einsum('bsd,df->bsf', h, w_up)
    return jnp.einsum('bsf,fd->bsd', hid, w_down)


def setup_inputs(seed: int = 0) -> dict:
    key = jax.random.key(seed)
    ks = jax.random.split(key, 24)
    f32 = jnp.float32
    L, D, F = DEPTH, D_MODEL, D_FF

    def nrm(k, shape, scale):
        return jax.random.normal(k, shape, f32) * scale

    x = nrm(ks[0], (BATCH, SEQ, D), 1.0)
    offset = jax.random.randint(ks[1], (BATCH, 1), 0, MAX_POS_OFFSET, dtype=jnp.int32)
    positions = offset + jnp.arange(SEQ, dtype=jnp.int32)[None, :]
    w_in = nrm(ks[2], (L, D, N_IN), D ** -0.5)
    b_glu = nrm(ks[3], (L, GLU_COLS), 0.02)
    conv_w = nrm(ks[4], (L, CONV_TAPS, CONV_WIDTH), CONV_TAPS ** -0.5)
    conv_b = nrm(ks[5], (L, CONV_WIDTH), 0.02)
    conv_ln_g = 1.0 + nrm(ks[6], (L, CONV_WIDTH), 0.02)
    conv_ln_b = nrm(ks[7], (L, CONV_WIDTH), 0.02)
    lam_q1 = nrm(ks[8], (L, HEAD_DIM), 0.1)
    lam_k1 = nrm(ks[9], (L, HEAD_DIM), 0.1)
    lam_q2 = nrm(ks[10], (L, HEAD_DIM), 0.1)
    lam_k2 = nrm(ks[11], (L, HEAD_DIM), 0.1)
    subln_g = 1.0 + nrm(ks[12], (L, V_HEAD_DIM), 0.02)
    w_o = nrm(ks[13], (L, MIX_WIDTH, D), MIX_WIDTH ** -0.5 * DEEPNORM_BETA)
    ln1_g = 1.0 + nrm(ks[14], (L, D), 0.02)
    ln1_b = nrm(ks[15], (L, D), 0.02)
    w_gate = nrm(ks[16], (L, D, F), D ** -0.5)
    w_up = nrm(ks[17], (L, D, F), D ** -0.5)
    w_down = nrm(ks[18], (L, F, D), F ** -0.5 * DEEPNORM_BETA)
    ln2_g = 1.0 + nrm(ks[19], (L, D), 0.02)
    ln2_b = nrm(ks[20], (L, D), 0.02)
    return {'x': x, 'positions': positions, 'w_in': w_in, 'b_glu': b_glu,
            'conv_w': conv_w, 'conv_b': conv_b, 'conv_ln_g': conv_ln_g, 'conv_ln_b': conv_ln_b,
            'lam_q1': lam_q1, 'lam_k1': lam_k1, 'lam_q2': lam_q2, 'lam_k2': lam_k2,
            'subln_g': subln_g, 'w_o': w_o, 'ln1_g': ln1_g, 'ln1_b': ln1_b,
            'w_gate': w_gate, 'w_up': w_up, 'w_down': w_down, 'ln2_g': ln2_g, 'ln2_b': ln2_b}


def reference(x, positions, w_in, b_glu, conv_w, conv_b, conv_ln_g, conv_ln_b,
              lam_q1, lam_k1, lam_q2, lam_k2, subln_g, w_o, ln1_g, ln1_b,
              w_gate, w_up, w_down, ln2_g, ln2_b):
    inv_freq = ROPE_THETA ** (-jnp.arange(0, ROPE_DIM, 2, dtype=jnp.float32) / ROPE_DIM)
    ang = positions.astype(jnp.float32)[..., None] * inv_freq
    cos = jnp.cos(ang).astype(x.dtype)
    sin = jnp.sin(ang).astype(x.dtype)
    for l in range(DEPTH):
        lambda_init = 0.8 - 0.6 * math.exp(-0.3 * l)
        mix = hybrid_mixer(x, cos, sin, w_in[l], b_glu[l], conv_w[l], conv_b[l],
                           conv_ln_g[l], conv_ln_b[l], lam_q1[l], lam_k1[l], lam_q2[l], lam_k2[l],
                           subln_g[l], w_o[l], lambda_init)
        h = layer_norm(DEEPNORM_ALPHA * x + mix, ln1_g[l], ln1_b[l])
        x = layer_norm(DEEPNORM_ALPHA * h + swiglu_ffn(h, w_gate[l], w_up[l], w_down[l]), ln2_g[l], ln2_b[l])
    return x
```

```python
import functools
import math

import jax
import jax.numpy as jnp
from jax import lax
from jax.experimental import pallas as pl
from jax.experimental.pallas import tpu as pltpu

D_MODEL = 4096
DEPTH = 1
ATTN_WIDTH = D_MODEL // 2
CONV_WIDTH = D_MODEL - ATTN_WIDTH
HEAD_DIM = 128
V_HEAD_DIM = 2 * HEAD_DIM
N_DIFF_HEADS = ATTN_WIDTH // V_HEAD_DIM
CONV_TAPS = 31
ROPE_THETA = 500000.0
ROPE_DIM = HEAD_DIM // 4
ROPE_HALF = ROPE_DIM // 2
D_FF = -(-8 * D_MODEL // (3 * 256)) * 256
Q_COLS = N_DIFF_HEADS * 2 * HEAD_DIM
K_COLS = Q_COLS
V_COLS = N_DIFF_HEADS * V_HEAD_DIM
QKV_COLS = Q_COLS + K_COLS + V_COLS
GLU_COLS = 2 * CONV_WIDTH
DEEPNORM_ALPHA = (2 * DEPTH) ** 0.25
LN_EPS = 1e-5

V7X_LANES = 128
V7X_SUBLANES = 8
V7X_VMEM_BYTES = 64 * 1024 * 1024
V7X_VMEM_REQUEST_CAP = 60000 * 1024

F32 = jnp.float32
BF16 = jnp.bfloat16
NEG_BIG = -1e30


def _nbytes(shape, dtype):
    return math.prod(shape) * jnp.dtype(dtype).itemsize


def _vmem_limit(pipelined, resident=()):
    need = 2 * sum(_nbytes(s, d) for s, d in pipelined) + sum(_nbytes(s, d) for s, d in resident)
    return min(max(need + need // 4, 16 * 1024 * 1024), V7X_VMEM_REQUEST_CAP)


def _dot(a, b):
    return jnp.dot(a, b, preferred_element_type=F32)


def _layer_norm_rows(v, g, b):
    mu = jnp.mean(v, axis=-1, keepdims=True)
    vc = v - mu
    var = jnp.mean(vc * vc, axis=-1, keepdims=True)
    return vc * lax.rsqrt(var + LN_EPS) * g + b


LN_CHUNK_ROWS = 64


def _layer_norm_block(src_ref, g_ref, b_ref, dst_refs):
    g = g_ref[...]
    b = b_ref[...]

    def chunk(c, carry):
        rows = pl.ds(pl.multiple_of(c * LN_CHUNK_ROWS, LN_CHUNK_ROWS), LN_CHUNK_ROWS)
        y = _layer_norm_rows(src_ref[rows, :], g, b)
        for dst in dst_refs:
            dst[rows, :] = y.astype(dst.dtype)
        return carry

    lax.fori_loop(0, src_ref.shape[0] // LN_CHUNK_ROWS, chunk, 0)


def _rope_table_kernel(pos_ref, inv_ref, cos_ref, sin_ref):
    ang = pos_ref[...].astype(F32) * inv_ref[...]
    lane = lax.broadcasted_iota(jnp.int32, ang.shape, 1)
    s = jnp.sin(ang)
    cos_ref[...] = jnp.cos(ang)
    sin_ref[...] = jnp.where(lane < ROPE_HALF, -s, s)


def _rope_tables(positions, m):
    tm = 1024
    inv_freq = ROPE_THETA ** (-jnp.arange(0, ROPE_DIM, 2, dtype=F32) / ROPE_DIM)
    inv_lanes = jnp.concatenate([inv_freq, inv_freq, jnp.zeros((V7X_LANES - ROPE_DIM,), F32)]).reshape(1, V7X_LANES)
    out = jax.ShapeDtypeStruct((m, V7X_LANES), F32)
    return pl.pallas_call(
        _rope_table_kernel,
        grid=(m // tm,),
        in_specs=[pl.BlockSpec((tm, 1), lambda i: (i, 0)),
                  pl.BlockSpec((1, V7X_LANES), lambda i: (0, 0))],
        out_specs=[pl.BlockSpec((tm, V7X_LANES), lambda i: (i, 0))] * 2,
        out_shape=[out, out],
        name="rope_tables",
    )(positions.reshape(m, 1), inv_lanes)


def _qkv_kernel(x_ref, w_ref, cos_ref, sin_ref, o_ref, *, tn, n_rot_tiles):
    j = pl.program_id(1)
    acc = _dot(x_ref[...], w_ref[...])

    @pl.when(j < n_rot_tiles)
    def _():
        c = cos_ref[...]
        s = sin_ref[...]
        lane = lax.broadcasted_iota(jnp.int32, c.shape, 1)
        for g in range(tn // V7X_LANES):
            t = acc[:, g * V7X_LANES:(g + 1) * V7X_LANES]
            partner = jnp.where(lane < ROPE_HALF,
                                pltpu.roll(t, V7X_LANES - ROPE_HALF, 1),
                                pltpu.roll(t, ROPE_HALF, 1))
            o_ref[:, g * V7X_LANES:(g + 1) * V7X_LANES] = (t * c + partner * s).astype(o_ref.dtype)

    @pl.when(j >= n_rot_tiles)
    def _():
        o_ref[...] = acc.astype(o_ref.dtype)


def _in_proj_qkv(x_bf, w_in_bf, cos_t, sin_t):
    m, d = x_bf.shape
    tm, tn = 1024, 1024
    kern = functools.partial(_qkv_kernel, tn=tn, n_rot_tiles=(Q_COLS + K_COLS) // tn)
    limit = _vmem_limit([((tm, d), BF16), ((d, tn), BF16), ((tm, tn), BF16), ((tm, V7X_LANES), F32), ((tm, V7X_LANES), F32)],
                        [((tm, tn), F32)])
    return pl.pallas_call(
        kern,
        grid=(m // tm, QKV_COLS // tn),
        in_specs=[pl.BlockSpec((tm, d), lambda i, j: (i, 0)),
                  pl.BlockSpec((d, tn), lambda i, j: (0, j)),
                  pl.BlockSpec((tm, V7X_LANES), lambda i, j: (i, 0)),
                  pl.BlockSpec((tm, V7X_LANES), lambda i, j: (i, 0))],
        out_specs=pl.BlockSpec((tm, tn), lambda i, j: (i, j)),
        out_shape=jax.ShapeDtypeStruct((m, QKV_COLS), BF16),
        compiler_params=pltpu.CompilerParams(dimension_semantics=("parallel", "arbitrary"), vmem_limit_bytes=limit),
        name="in_proj_qkv",
    )(x_bf, w_in_bf, cos_t, sin_t)


def _glu_kernel(x_ref, wa_ref, wg_ref, ba_ref, bg_ref, o_ref):
    x = x_ref[...]
    a = _dot(x, wa_ref[...]) + ba_ref[...]
    gate = _dot(x, wg_ref[...]) + bg_ref[...]
    o_ref[...] = (a * jax.nn.sigmoid(gate)).astype(o_ref.dtype)


def _in_proj_glu(x_bf, w_in_bf, b_glu):
    m, d = x_bf.shape
    tm, tn = 1024, 512
    a0 = QKV_COLS // tn
    g0 = (QKV_COLS + CONV_WIDTH) // tn
    nb = CONV_WIDTH // tn
    limit = _vmem_limit([((tm, d), BF16), ((d, tn), BF16), ((d, tn), BF16), ((tm, tn), F32)],
                        [((tm, tn), F32), ((tm, tn), F32)])
    return pl.pallas_call(
        _glu_kernel,
        grid=(m // tm, nb),
        in_specs=[pl.BlockSpec((tm, d), lambda i, j: (i, 0)),
                  pl.BlockSpec((d, tn), lambda i, j: (0, a0 + j)),
                  pl.BlockSpec((d, tn), lambda i, j: (0, g0 + j)),
                  pl.BlockSpec((1, tn), lambda i, j: (0, j)),
                  pl.BlockSpec((1, tn), lambda i, j: (0, nb + j))],
        out_specs=pl.BlockSpec((tm, tn), lambda i, j: (i, j)),
        out_shape=jax.ShapeDtypeStruct((m, CONV_WIDTH), F32),
        compiler_params=pltpu.CompilerParams(dimension_semantics=("parallel", "arbitrary"), vmem_limit_bytes=limit),
        name="in_proj_glu",
    )(x_bf, w_in_bf, w_in_bf, b_glu.reshape(1, GLU_COLS), b_glu.reshape(1, GLU_COLS))


CONV_HALO = 32
CONV_ROWS = 64


def _conv_ln_kernel(cur_ref, prev_ref, w_ref, cb_ref, g_ref, b_ref, o_ref, ext_ref, y_ref, *, ts):
    i = pl.program_id(1)
    width = cur_ref.shape[-1]
    ext_ref[0:CONV_HALO, :] = jnp.where(i > 0, prev_ref[0], 0.0)
    ext_ref[CONV_HALO:, :] = cur_ref[0]
    first = CONV_HALO - (CONV_TAPS - 1)

    def channel_group(cg, carry):
        c0 = pl.multiple_of(cg * V7X_LANES, V7X_LANES)
        lanes = pl.ds(c0, V7X_LANES)
        for r0 in range(0, ts, CONV_ROWS):
            acc = jnp.zeros((CONV_ROWS, V7X_LANES), F32)
            for t in range(CONV_TAPS):
                acc = acc + ext_ref[pl.ds(first + r0 + t, CONV_ROWS), lanes] * w_ref[pl.ds(t, 1), lanes]
            y_ref[pl.ds(r0, CONV_ROWS), lanes] = acc + cb_ref[:, lanes]
        return carry

    lax.fori_loop(0, width // V7X_LANES, channel_group, 0)
    z = _layer_norm_rows(y_ref[...], g_ref[...], b_ref[...])
    o_ref[0] = (z * jax.nn.sigmoid(z)).astype(o_ref.dtype)


def _conv_ln_swish(c, conv_w, conv_b, ln_g, ln_b):
    b, s, width = c.shape
    ts = 256
    per = ts // CONV_HALO
    kern = functools.partial(_conv_ln_kernel, ts=ts)
    row = lambda a: a.reshape(1, width)
    limit = _vmem_limit([((ts, width), F32), ((CONV_HALO, width), F32), ((CONV_TAPS, width), F32), ((ts, width), BF16)],
                        [((ts + CONV_HALO, width), F32), ((ts, width), F32), ((ts, width), F32)])
    return pl.pallas_call(
        kern,
        grid=(b, s // ts),
        in_specs=[pl.BlockSpec((1, ts, width), lambda bi, i: (bi, i, 0)),
                  pl.BlockSpec((1, CONV_HALO, width), lambda bi, i: (bi, jnp.maximum(i * per - 1, 0), 0)),
                  pl.BlockSpec((CONV_TAPS, width), lambda bi, i: (0, 0)),
                  pl.BlockSpec((1, width), lambda bi, i: (0, 0)),
                  pl.BlockSpec((1, width), lambda bi, i: (0, 0)),
                  pl.BlockSpec((1, width), lambda bi, i: (0, 0))],
        out_specs=pl.BlockSpec((1, ts, width), lambda bi, i: (bi, i, 0)),
        out_shape=jax.ShapeDtypeStruct((b, s, width), BF16),
        scratch_shapes=[pltpu.VMEM((ts + CONV_HALO, width), F32), pltpu.VMEM((ts, width), F32)],
        compiler_params=pltpu.CompilerParams(dimension_semantics=("parallel", "arbitrary"), vmem_limit_bytes=limit),
        name="conv_ln_swish",
    )(c, c, conv_w, row(conv_b), row(ln_g), row(ln_b))


ATTN_BLOCK = 256
SCORE_SCALE_LOG2 = HEAD_DIM ** -0.5 * math.log2(math.e)


def _attn_kernel(q1_ref, q2_ref, k1_ref, k2_ref, v_ref, lq1_ref, lk1_ref, lq2_ref, lk2_ref, g_ref, o_ref, *,
                 seq, lambda_init):
    blk = ATTN_BLOCK
    lam = (jnp.exp(jnp.sum(lq1_ref[...] * lk1_ref[...], axis=-1, keepdims=True))
           - jnp.exp(jnp.sum(lq2_ref[...] * lk2_ref[...], axis=-1, keepdims=True)) + lambda_init)
    row = lax.broadcasted_iota(jnp.int32, (blk, blk), 0)
    col = lax.broadcasted_iota(jnp.int32, (blk, blk), 1)
    causal = col <= row

    def scores(q, k_ref, c0, masked):
        s = lax.dot_general(q, k_ref[pl.ds(c0, blk), :], (((1,), (1,)), ((), ())),
                            preferred_element_type=F32) * SCORE_SCALE_LOG2
        return jnp.where(causal, s, NEG_BIG) if masked else s

    def online(s, v, m, l, acc):
        m_new = jnp.maximum(m, jnp.max(s, axis=-1, keepdims=True))
        alpha = jnp.exp2(m - m_new)
        p = jnp.exp2(s - m_new)
        l_new = alpha * l + jnp.sum(p, axis=-1, keepdims=True)
        acc_new = alpha * acc + _dot(p.astype(BF16), v)
        return m_new, l_new, acc_new

    def q_block(qi, carry):
        r0 = pl.multiple_of(qi * blk, blk)
        q1 = q1_ref[pl.ds(r0, blk), :]
        q2 = q2_ref[pl.ds(r0, blk), :]

        def kv_step(j, st, masked):
            c0 = pl.multiple_of(j * blk, blk)
            v = v_ref[pl.ds(c0, blk), :]
            m1, l1, a1, m2, l2, a2 = st
            m1, l1, a1 = online(scores(q1, k1_ref, c0, masked), v, m1, l1, a1)
            m2, l2, a2 = online(scores(q2, k2_ref, c0, masked), v, m2, l2, a2)
            return m1, l1, a1, m2, l2, a2

        neg = jnp.full((blk, 1), NEG_BIG, F32)
        zero = jnp.zeros((blk, 1), F32)
        zacc = jnp.zeros((blk, V_HEAD_DIM), F32)
        st = lax.fori_loop(0, qi, lambda j, st: kv_step(j, st, False), (neg, zero, zacc, neg, zero, zacc))
        m1, l1, a1, m2, l2, a2 = kv_step(qi, st, True)
        o = a1 / l1 - lam * (a2 / l2)
        y = o * lax.rsqrt(jnp.mean(o * o, axis=-1, keepdims=True) + LN_EPS)
        o_ref[pl.ds(r0, blk), :] = ((y * g_ref[...]) * (1.0 - lambda_init)).astype(o_ref.dtype)
        return carry

    lax.fori_loop(0, seq // blk, q_block, 0)


def _diff_attention(qkv, lam_q1, lam_k1, lam_q2, lam_k2, subln_g, batch, seq, lambda_init):
    m = qkv.shape[0]
    kern = functools.partial(_attn_kernel, seq=seq, lambda_init=lambda_init)
    kcol = Q_COLS // HEAD_DIM
    vcol = (Q_COLS + K_COLS) // V_HEAD_DIM
    qk_spec = lambda base, comp: pl.BlockSpec((seq, HEAD_DIM), lambda b, h: (b, base + 2 * h + comp))
    vec = lambda a: a.reshape(1, -1)
    vec_spec = lambda n: pl.BlockSpec((1, n), lambda b, h: (0, 0))
    limit = _vmem_limit([((seq, HEAD_DIM), BF16)] * 4 + [((seq, V_HEAD_DIM), BF16)] * 2,
                        [((ATTN_BLOCK, V_HEAD_DIM), F32)] * 8)
    return pl.pallas_call(
        kern,
        grid=(batch, N_DIFF_HEADS),
        in_specs=[qk_spec(0, 0), qk_spec(0, 1), qk_spec(kcol, 0), qk_spec(kcol, 1),
                  pl.BlockSpec((seq, V_HEAD_DIM), lambda b, h: (b, vcol + h)),
                  vec_spec(HEAD_DIM), vec_spec(HEAD_DIM), vec_spec(HEAD_DIM), vec_spec(HEAD_DIM),
                  vec_spec(V_HEAD_DIM)],
        out_specs=pl.BlockSpec((seq, V_HEAD_DIM), lambda b, h: (b, h)),
        out_shape=jax.ShapeDtypeStruct((m, ATTN_WIDTH), BF16),
        compiler_params=pltpu.CompilerParams(dimension_semantics=("parallel", "parallel"), vmem_limit_bytes=limit),
        name="diff_attention",
    )(qkv, qkv, qkv, qkv, qkv, vec(lam_q1), vec(lam_k1), vec(lam_q2), vec(lam_k2), vec(subln_g))


def _oproj_ln_kernel(a_ref, c_ref, wa_ref, wc_ref, x_ref, g_ref, b_ref, h_ref, hb_ref, *, tn, nj):
    j = pl.program_id(1)
    y = _dot(a_ref[...], wa_ref[...]) + _dot(c_ref[...], wc_ref[...]) + DEEPNORM_ALPHA * x_ref[...]
    h_ref[:, pl.ds(pl.multiple_of(j * tn, tn), tn)] = y

    @pl.when(j == nj - 1)
    def _():
        _layer_norm_block(h_ref, g_ref, b_ref, (h_ref, hb_ref))


def _out_proj_ln(attn, conv, w_o_bf, x2, ln_g, ln_b):
    m, d = x2.shape
    tm, tn = 512, 512
    nj = d // tn
    kern = functools.partial(_oproj_ln_kernel, tn=tn, nj=nj)
    limit = _vmem_limit([((tm, ATTN_WIDTH), BF16), ((tm, CONV_WIDTH), BF16), ((ATTN_WIDTH, tn), BF16),
                         ((CONV_WIDTH, tn), BF16), ((tm, tn), F32), ((tm, d), F32), ((tm, d), BF16)],
                        [((tm, tn), F32)] * 2)
    return pl.pallas_call(
        kern,
        grid=(m // tm, nj),
        in_specs=[pl.BlockSpec((tm, ATTN_WIDTH), lambda i, j: (i, 0)),
                  pl.BlockSpec((tm, CONV_WIDTH), lambda i, j: (i, 0)),
                  pl.BlockSpec((ATTN_WIDTH, tn), lambda i, j: (0, j)),
                  pl.BlockSpec((CONV_WIDTH, tn), lambda i, j: (1, j)),
                  pl.BlockSpec((tm, tn), lambda i, j: (i, j)),
                  pl.BlockSpec((1, d), lambda i, j: (0, 0)),
                  pl.BlockSpec((1, d), lambda i, j: (0, 0))],
        out_specs=[pl.BlockSpec((tm, d), lambda i, j: (i, 0)),
                   pl.BlockSpec((tm, d), lambda i, j: (i, 0))],
        out_shape=[jax.ShapeDtypeStruct((m, d), F32), jax.ShapeDtypeStruct((m, d), BF16)],
        compiler_params=pltpu.CompilerParams(dimension_semantics=("parallel", "arbitrary"), vmem_limit_bytes=limit),
        name="out_proj_ln",
    )(attn, conv, w_o_bf, w_o_bf, x2, ln_g.reshape(1, d), ln_b.reshape(1, d))


FF_TILE = 512
D_FF_PAD = -(-D_FF // FF_TILE) * FF_TILE


def _ffn_up_kernel(h_ref, wg_ref, wu_ref, o_ref, *, tn):
    j = pl.program_id(1)
    h = h_ref[...]
    g = _dot(h, wg_ref[...])
    u = _dot(h, wu_ref[...])
    hid = (g * jax.nn.sigmoid(g)) * u
    col = j * tn + lax.broadcasted_iota(jnp.int32, hid.shape, 1)
    o_ref[...] = jnp.where(col < D_FF, hid, 0.0).astype(o_ref.dtype)


def _ffn_up(h_bf, w_gate_bf, w_up_bf):
    m, d = h_bf.shape
    tm, tn = 1024, FF_TILE
    kern = functools.partial(_ffn_up_kernel, tn=tn)
    limit = _vmem_limit([((tm, d), BF16), ((d, tn), BF16), ((d, tn), BF16), ((tm, tn), BF16)],
                        [((tm, tn), F32)] * 3)
    return pl.pallas_call(
        kern,
        grid=(m // tm, D_FF_PAD // tn),
        in_specs=[pl.BlockSpec((tm, d), lambda i, j: (i, 0)),
                  pl.BlockSpec((d, tn), lambda i, j: (0, j)),
                  pl.BlockSpec((d, tn), lambda i, j: (0, j))],
        out_specs=pl.BlockSpec((tm, tn), lambda i, j: (i, j)),
        out_shape=jax.ShapeDtypeStruct((m, D_FF_PAD), BF16),
        compiler_params=pltpu.CompilerParams(dimension_semantics=("parallel", "arbitrary"), vmem_limit_bytes=limit),
        name="ffn_up",
    )(h_bf, w_gate_bf, w_up_bf)


DOWN_COL_CHUNK = 1024


def _ffn_down_ln_kernel(hid_ref, w_ref, h_ref, g_ref, b_ref, o_ref, *, nk):
    k = pl.program_id(1)

    @pl.when(k == 0)
    def _():
        o_ref[...] = DEEPNORM_ALPHA * h_ref[...]

    hid = hid_ref[...]
    for n0 in range(0, o_ref.shape[1], DOWN_COL_CHUNK):
        o_ref[:, n0:n0 + DOWN_COL_CHUNK] += _dot(hid, w_ref[:, n0:n0 + DOWN_COL_CHUNK])

    @pl.when(k == nk - 1)
    def _():
        _layer_norm_block(o_ref, g_ref, b_ref, (o_ref,))


def _ffn_down_ln(hid, w_down_bf, h, ln_g, ln_b):
    m, d = h.shape
    tm, tk = 512, 1024
    nk = D_FF_PAD // tk
    kern = functools.partial(_ffn_down_ln_kernel, nk=nk)
    limit = _vmem_limit([((tm, tk), BF16), ((tk, d), BF16), ((tm, d), F32), ((tm, d), F32)],
                        [((tm, DOWN_COL_CHUNK), F32)])
    return pl.pallas_call(
        kern,
        grid=(m // tm, nk),
        in_specs=[pl.BlockSpec((tm, tk), lambda i, k: (i, k)),
                  pl.BlockSpec((tk, d), lambda i, k: (k, 0)),
                  pl.BlockSpec((tm, d), lambda i, k: (i, 0)),
                  pl.BlockSpec((1, d), lambda i, k: (0, 0)),
                  pl.BlockSpec((1, d), lambda i, k: (0, 0))],
        out_specs=pl.BlockSpec((tm, d), lambda i, k: (i, 0)),
        out_shape=jax.ShapeDtypeStruct((m, d), F32),
        compiler_params=pltpu.CompilerParams(dimension_semantics=("parallel", "arbitrary"), vmem_limit_bytes=limit),
        name="ffn_down_ln",
    )(hid, w_down_bf, h, ln_g.reshape(1, d), ln_b.reshape(1, d))


def kernel(x, positions, w_in, b_glu, conv_w, conv_b, conv_ln_g, conv_ln_b, lam_q1, lam_k1, lam_q2, lam_k2,
           subln_g, w_o, ln1_g, ln1_b, w_gate, w_up, w_down, ln2_g, ln2_b):
    batch, seq, d = x.shape
    m = batch * seq
    cos_t, sin_t = _rope_tables(positions, m)
    x2 = x.reshape(m, d)
    for l in range(DEPTH):
        lambda_init = 0.8 - 0.6 * math.exp(-0.3 * l)
        w_in_bf = w_in[l].astype(BF16)
        w_o_bf = w_o[l].astype(BF16)
        w_gate_bf = w_gate[l].astype(BF16)
        w_up_bf = w_up[l].astype(BF16)
        w_down_bf = jnp.pad(w_down[l].astype(BF16), ((0, D_FF_PAD - D_FF), (0, 0)))
        x_bf = x2.astype(BF16)

        qkv = _in_proj_qkv(x_bf, w_in_bf, cos_t, sin_t)
        glu = _in_proj_glu(x_bf, w_in_bf, b_glu[l])
        conv = _conv_ln_swish(glu.reshape(batch, seq, CONV_WIDTH), conv_w[l], conv_b[l], conv_ln_g[l], conv_ln_b[l])
        attn = _diff_attention(qkv, lam_q1[l], lam_k1[l], lam_q2[l], lam_k2[l], subln_g[l], batch, seq, lambda_init)
        h, h_bf = _out_proj_ln(attn, conv.reshape(m, CONV_WIDTH), w_o_bf, x2, ln1_g[l], ln1_b[l])
        hid = _ffn_up(h_bf, w_gate_bf, w_up_bf)
        x2 = _ffn_down_ln(hid, w_down_bf, h, ln2_g[l], ln2_b[l])
    return x2.reshape(batch, seq, d)
```

```python
import functools
import math

import jax
import jax.numpy as jnp
from jax import lax
from jax.experimental import pallas as pl
from jax.experimental.pallas import tpu as pltpu

D_MODEL = 4096
DEPTH = 1
ATTN_WIDTH = D_MODEL // 2
CONV_WIDTH = D_MODEL - ATTN_WIDTH
HEAD_DIM = 128
V_HEAD_DIM = 2 * HEAD_DIM
N_DIFF_HEADS = ATTN_WIDTH // V_HEAD_DIM
CONV_TAPS = 31
ROPE_THETA = 500000.0
ROPE_DIM = HEAD_DIM // 4
ROPE_HALF = ROPE_DIM // 2
D_FF = -(-8 * D_MODEL // (3 * 256)) * 256
Q_COLS = N_DIFF_HEADS * 2 * HEAD_DIM
K_COLS = Q_COLS
V_COLS = N_DIFF_HEADS * V_HEAD_DIM
QKV_COLS = Q_COLS + K_COLS + V_COLS
GLU_COLS = 2 * CONV_WIDTH
DEEPNORM_ALPHA = (2 * DEPTH) ** 0.25
LN_EPS = 1e-5

V7X_LANES = 128
V7X_SUBLANES = 8
V7X_VMEM_BYTES = 64 * 1024 * 1024
V7X_VMEM_REQUEST_CAP = 60000 * 1024

F32 = jnp.float32
BF16 = jnp.bfloat16
NEG_BIG = -1e30


def _nbytes(shape, dtype):
    return math.prod(shape) * jnp.dtype(dtype).itemsize


def _vmem_limit(pipelined, resident=()):
    need = 2 * sum(_nbytes(s, d) for s, d in pipelined) + sum(_nbytes(s, d) for s, d in resident)
    return min(max(need + need // 4, 16 * 1024 * 1024), V7X_VMEM_REQUEST_CAP)


def _dot(a, b):
    return jnp.dot(a, b, preferred_element_type=F32)


def _layer_norm_rows(v, g, b):
    mu = jnp.mean(v, axis=-1, keepdims=True)
    vc = v - mu
    var = jnp.mean(vc * vc, axis=-1, keepdims=True)
    return vc * lax.rsqrt(var + LN_EPS) * g + b


LN_CHUNK_ROWS = 64


def _layer_norm_block(src_ref, g_ref, b_ref, dst_refs):
    g = g_ref[...]
    b = b_ref[...]

    def chunk(c, carry):
        rows = pl.ds(pl.multiple_of(c * LN_CHUNK_ROWS, LN_CHUNK_ROWS), LN_CHUNK_ROWS)
        y = _layer_norm_rows(src_ref[rows, :], g, b)
        for dst in dst_refs:
            dst[rows, :] = y.astype(dst.dtype)
        return carry

    lax.fori_loop(0, src_ref.shape[0] // LN_CHUNK_ROWS, chunk, 0)


def _rope_table_kernel(pos_ref, inv_ref, cos_ref, sin_ref):
    ang = pos_ref[...].astype(F32) * inv_ref[...]
    lane = lax.broadcasted_iota(jnp.int32, ang.shape, 1)
    s = jnp.sin(ang)
    cos_ref[...] = jnp.cos(ang)
    sin_ref[...] = jnp.where(lane < ROPE_HALF, -s, s)


def _rope_tables(positions, m):
    tm = 1024
    inv_freq = ROPE_THETA ** (-jnp.arange(0, ROPE_DIM, 2, dtype=F32) / ROPE_DIM)
    inv_lanes = jnp.concatenate([inv_freq, inv_freq, jnp.zeros((V7X_LANES - ROPE_DIM,), F32)]).reshape(1, V7X_LANES)
    out = jax.ShapeDtypeStruct((m, V7X_LANES), F32)
    return pl.pallas_call(
        _rope_table_kernel,
        grid=(m // tm,),
        in_specs=[pl.BlockSpec((tm, 1), lambda i: (i, 0)),
                  pl.BlockSpec((1, V7X_LANES), lambda i: (0, 0))],
        out_specs=[pl.BlockSpec((tm, V7X_LANES), lambda i: (i, 0))] * 2,
        out_shape=[out, out],
        name="rope_tables",
    )(positions.reshape(m, 1), inv_lanes)


def _qkv_kernel(x_ref, w_ref, cos_ref, sin_ref, o_ref, *, tn, n_rot_tiles):
    j = pl.program_id(1)
    acc = _dot(x_ref[...], w_ref[...])

    @pl.when(j < n_rot_tiles)
    def _():
        c = cos_ref[...]
        s = sin_ref[...]
        lane = lax.broadcasted_iota(jnp.int32, c.shape, 1)
        for g in range(tn // V7X_LANES):
            t = acc[:, g * V7X_LANES:(g + 1) * V7X_LANES]
            partner = jnp.where(lane < ROPE_HALF,
                                pltpu.roll(t, V7X_LANES - ROPE_HALF, 1),
                                pltpu.roll(t, ROPE_HALF, 1))
            o_ref[:, g * V7X_LANES:(g + 1) * V7X_LANES] = (t * c + partner * s).astype(o_ref.dtype)

    @pl.when(j >= n_rot_tiles)
    def _():
        o_ref[...] = acc.astype(o_ref.dtype)


def _in_proj_qkv(x_bf, w_in_bf, cos_t, sin_t):
    m, d = x_bf.shape
    tm, tn = 1024, 1024
    kern = functools.partial(_qkv_kernel, tn=tn, n_rot_tiles=(Q_COLS + K_COLS) // tn)
    limit = _vmem_limit([((tm, d), BF16), ((d, tn), BF16), ((tm, tn), BF16), ((tm, V7X_LANES), F32), ((tm, V7X_LANES), F32)],
                        [((tm, tn), F32)])
    return pl.pallas_call(
        kern,
        grid=(m // tm, QKV_COLS // tn),
        in_specs=[pl.BlockSpec((tm, d), lambda i, j: (i, 0)),
                  pl.BlockSpec((d, tn), lambda i, j: (0, j)),
                  pl.BlockSpec((tm, V7X_LANES), lambda i, j: (i, 0)),
                  pl.BlockSpec((tm, V7X_LANES), lambda i, j: (i, 0))],
        out_specs=pl.BlockSpec((tm, tn), lambda i, j: (i, j)),
        out_shape=jax.ShapeDtypeStruct((m, QKV_COLS), BF16),
        compiler_params=pltpu.CompilerParams(dimension_semantics=("parallel", "arbitrary"), vmem_limit_bytes=limit),
        name="in_proj_qkv",
    )(x_bf, w_in_bf, cos_t, sin_t)


def _glu_kernel(x_ref, wa_ref, wg_ref, ba_ref, bg_ref, o_ref):
    x = x_ref[...]
    a = _dot(x, wa_ref[...]) + ba_ref[...]
    gate = _dot(x, wg_ref[...]) + bg_ref[...]
    o_ref[...] = (a * jax.nn.sigmoid(gate)).astype(o_ref.dtype)


def _in_proj_glu(x_bf, w_in_bf, b_glu):
    m, d = x_bf.shape
    tm, tn = 1024, 512
    a0 = QKV_COLS // tn
    g0 = (QKV_COLS + CONV_WIDTH) // tn
    nb = CONV_WIDTH // tn
    limit = _vmem_limit([((tm, d), BF16), ((d, tn), BF16), ((d, tn), BF16), ((tm, tn), F32)],
                        [((tm, tn), F32), ((tm, tn), F32)])
    return pl.pallas_call(
        _glu_kernel,
        grid=(m // tm, nb),
        in_specs=[pl.BlockSpec((tm, d), lambda i, j: (i, 0)),
                  pl.BlockSpec((d, tn), lambda i, j: (0, a0 + j)),
                  pl.BlockSpec((d, tn), lambda i, j: (0, g0 + j)),
                  pl.BlockSpec((1, tn), lambda i, j: (0, j)),
                  pl.BlockSpec((1, tn), lambda i, j: (0, nb + j))],
        out_specs=pl.BlockSpec((tm, tn), lambda i, j: (i, j)),
        out_shape=jax.ShapeDtypeStruct((m, CONV_WIDTH), F32),
        compiler_params=pltpu.CompilerParams(dimension_semantics=("parallel", "arbitrary"), vmem_limit_bytes=limit),
        name="in_proj_glu",
    )(x_bf, w_in_bf, w_in_bf, b_glu.reshape(1, GLU_COLS), b_glu.reshape(1, GLU_COLS))


CONV_HALO = 32
CONV_ROWS = 128
CONV_BACK = V7X_SUBLANES * ((CONV_TAPS - 1) // V7X_SUBLANES)


def _conv_ln_kernel(cur_ref, prev_ref, w_ref, cb_ref, g_ref, b_ref, o_ref, ext_ref, y_ref, *, ts):
    i = pl.program_id(1)
    width = cur_ref.shape[-1]
    ext_ref[0:CONV_HALO, :] = jnp.where(i > 0, prev_ref[0], 0.0)
    ext_ref[CONV_HALO:, :] = cur_ref[0]

    def channel_group(cg, carry):
        c0 = pl.multiple_of(cg * V7X_LANES, V7X_LANES)
        lanes = pl.ds(c0, V7X_LANES)
        for r0 in range(0, ts, CONV_ROWS):
            acc = jnp.zeros((CONV_ROWS, V7X_LANES), F32)
            win = ext_ref[pl.ds(r0, CONV_ROWS + CONV_HALO), lanes]
            for r in range(V7X_SUBLANES):
                shifted = pltpu.roll(win, r, 0) if r else win
                for q in range(CONV_BACK // V7X_SUBLANES + 1):
                    u = V7X_SUBLANES * q + r
                    if u < CONV_TAPS:
                        lo = CONV_HALO - V7X_SUBLANES * q
                        acc = acc + shifted[lo:lo + CONV_ROWS, :] * w_ref[pl.ds(CONV_TAPS - 1 - u, 1), lanes]
            y_ref[pl.ds(r0, CONV_ROWS), lanes] = acc + cb_ref[:, lanes]
        return carry

    lax.fori_loop(0, width // V7X_LANES, channel_group, 0)
    z = _layer_norm_rows(y_ref[...], g_ref[...], b_ref[...])
    o_ref[0] = (z * jax.nn.sigmoid(z)).astype(o_ref.dtype)


def _conv_ln_swish(c, conv_w, conv_b, ln_g, ln_b):
    b, s, width = c.shape
    ts = 256
    per = ts // CONV_HALO
    kern = functools.partial(_conv_ln_kernel, ts=ts)
    row = lambda a: a.reshape(1, width)
    limit = _vmem_limit([((ts, width), F32), ((CONV_HALO, width), F32), ((CONV_TAPS, width), F32), ((ts, width), BF16)],
                        [((ts + CONV_HALO, width), F32), ((ts, width), F32), ((ts, width), F32)])
    return pl.pallas_call(
        kern,
        grid=(b, s // ts),
        in_specs=[pl.BlockSpec((1, ts, width), lambda bi, i: (bi, i, 0)),
                  pl.BlockSpec((1, CONV_HALO, width), lambda bi, i: (bi, jnp.maximum(i * per - 1, 0), 0)),
                  pl.BlockSpec((CONV_TAPS, width), lambda bi, i: (0, 0)),
                  pl.BlockSpec((1, width), lambda bi, i: (0, 0)),
                  pl.BlockSpec((1, width), lambda bi, i: (0, 0)),
                  pl.BlockSpec((1, width), lambda bi, i: (0, 0))],
        out_specs=pl.BlockSpec((1, ts, width), lambda bi, i: (bi, i, 0)),
        out_shape=jax.ShapeDtypeStruct((b, s, width), BF16),
        scratch_shapes=[pltpu.VMEM((ts + CONV_HALO, width), F32), pltpu.VMEM((ts, width), F32)],
        compiler_params=pltpu.CompilerParams(dimension_semantics=("parallel", "arbitrary"), vmem_limit_bytes=limit),
        name="conv_ln_swish",
    )(c, c, conv_w, row(conv_b), row(ln_g), row(ln_b))


ATTN_BLOCK = 512
SCORE_SCALE_LOG2 = HEAD_DIM ** -0.5 * math.log2(math.e)


def _dot_nt(a, b):
    return lax.dot_general(a, b, (((1,), (1,)), ((), ())), preferred_element_type=F32)


def _attn_kernel(q1_ref, q2_ref, k1_ref, k2_ref, v_ref, lq1_ref, lk1_ref, lq2_ref, lk2_ref, g_ref, o_ref, *,
                 seq, lambda_init):
    blk = ATTN_BLOCK
    lam = (jnp.exp(jnp.sum(lq1_ref[...] * lk1_ref[...], axis=-1, keepdims=True))
           - jnp.exp(jnp.sum(lq2_ref[...] * lk2_ref[...], axis=-1, keepdims=True)) + lambda_init)
    row = lax.broadcasted_iota(jnp.int32, (blk, blk), 0)
    col = lax.broadcasted_iota(jnp.int32, (blk, blk), 1)
    causal = col <= row

    def softmax_times_v(q, k_ref, r0):
        s_diag = jnp.where(causal, _dot_nt(q, k_ref[r0:r0 + blk, :]), NEG_BIG)
        m = jnp.max(s_diag, axis=-1, keepdims=True)
        if r0:
            s_past = _dot_nt(q, k_ref[0:r0, :])
            m = jnp.maximum(m, jnp.max(s_past, axis=-1, keepdims=True))
        p = jnp.exp2((s_diag - m) * SCORE_SCALE_LOG2)
        l = jnp.sum(p, axis=-1, keepdims=True)
        acc = _dot(p.astype(BF16), v_ref[r0:r0 + blk, :])
        if r0:
            p = jnp.exp2((s_past - m) * SCORE_SCALE_LOG2)
            l = l + jnp.sum(p, axis=-1, keepdims=True)
            acc = acc + _dot(p.astype(BF16), v_ref[0:r0, :])
        return acc / l

    for r0 in range(0, seq, blk):
        o = (softmax_times_v(q1_ref[r0:r0 + blk, :], k1_ref, r0)
             - lam * softmax_times_v(q2_ref[r0:r0 + blk, :], k2_ref, r0))
        y = o * lax.rsqrt(jnp.mean(o * o, axis=-1, keepdims=True) + LN_EPS)
        o_ref[r0:r0 + blk, :] = ((y * g_ref[...]) * (1.0 - lambda_init)).astype(o_ref.dtype)


def _diff_attention(qkv, lam_q1, lam_k1, lam_q2, lam_k2, subln_g, batch, seq, lambda_init):
    m = qkv.shape[0]
    kern = functools.partial(_attn_kernel, seq=seq, lambda_init=lambda_init)
    kcol = Q_COLS // HEAD_DIM
    vcol = (Q_COLS + K_COLS) // V_HEAD_DIM
    qk_spec = lambda base, comp: pl.BlockSpec((seq, HEAD_DIM), lambda b, h: (b, base + 2 * h + comp))
    vec = lambda a: a.reshape(1, -1)
    vec_spec = lambda n: pl.BlockSpec((1, n), lambda b, h: (0, 0))
    limit = _vmem_limit([((seq, HEAD_DIM), BF16)] * 4 + [((seq, V_HEAD_DIM), BF16)] * 2,
                        [((ATTN_BLOCK, seq), F32)] * 3 + [((ATTN_BLOCK, seq), BF16)] * 2)
    return pl.pallas_call(
        kern,
        grid=(batch, N_DIFF_HEADS),
        in_specs=[qk_spec(0, 0), qk_spec(0, 1), qk_spec(kcol, 0), qk_spec(kcol, 1),
                  pl.BlockSpec((seq, V_HEAD_DIM), lambda b, h: (b, vcol + h)),
                  vec_spec(HEAD_DIM), vec_spec(HEAD_DIM), vec_spec(HEAD_DIM), vec_spec(HEAD_DIM),
                  vec_spec(V_HEAD_DIM)],
        out_specs=pl.BlockSpec((seq, V_HEAD_DIM), lambda b, h: (b, h)),
        out_shape=jax.ShapeDtypeStruct((m, ATTN_WIDTH), BF16),
        compiler_params=pltpu.CompilerParams(dimension_semantics=("parallel", "parallel"), vmem_limit_bytes=limit),
        name="diff_attention",
    )(qkv, qkv, qkv, qkv, qkv, vec(lam_q1), vec(lam_k1), vec(lam_q2), vec(lam_k2), vec(subln_g))


def _oproj_ln_kernel(a_ref, c_ref, wa_ref, wc_ref, x_ref, g_ref, b_ref, h_ref, hb_ref, *, tn, nj):
    j = pl.program_id(1)
    y = _dot(a_ref[...], wa_ref[...]) + _dot(c_ref[...], wc_ref[...]) + DEEPNORM_ALPHA * x_ref[...]
    h_ref[:, pl.ds(pl.multiple_of(j * tn, tn), tn)] = y

    @pl.when(j == nj - 1)
    def _():
        _layer_norm_block(h_ref, g_ref, b_ref, (h_ref, hb_ref))


def _out_proj_ln(attn, conv, w_o_bf, x2, ln_g, ln_b):
    m, d = x2.shape
    tm, tn = 512, 512
    nj = d // tn
    kern = functools.partial(_oproj_ln_kernel, tn=tn, nj=nj)
    limit = _vmem_limit([((tm, ATTN_WIDTH), BF16), ((tm, CONV_WIDTH), BF16), ((ATTN_WIDTH, tn), BF16),
                         ((CONV_WIDTH, tn), BF16), ((tm, tn), F32), ((tm, d), F32), ((tm, d), BF16)],
                        [((tm, tn), F32)] * 2)
    return pl.pallas_call(
        kern,
        grid=(m // tm, nj),
        in_specs=[pl.BlockSpec((tm, ATTN_WIDTH), lambda i, j: (i, 0)),
                  pl.BlockSpec((tm, CONV_WIDTH), lambda i, j: (i, 0)),
                  pl.BlockSpec((ATTN_WIDTH, tn), lambda i, j: (0, j)),
                  pl.BlockSpec((CONV_WIDTH, tn), lambda i, j: (1, j)),
                  pl.BlockSpec((tm, tn), lambda i, j: (i, j)),
                  pl.BlockSpec((1, d), lambda i, j: (0, 0)),
                  pl.BlockSpec((1, d), lambda i, j: (0, 0))],
        out_specs=[pl.BlockSpec((tm, d), lambda i, j: (i, 0)),
                   pl.BlockSpec((tm, d), lambda i, j: (i, 0))],
        out_shape=[jax.ShapeDtypeStruct((m, d), F32), jax.ShapeDtypeStruct((m, d), BF16)],
        compiler_params=pltpu.CompilerParams(dimension_semantics=("parallel", "arbitrary"), vmem_limit_bytes=limit),
        name="out_proj_ln",
    )(attn, conv, w_o_bf, w_o_bf, x2, ln_g.reshape(1, d), ln_b.reshape(1, d))


FF_TILE = 512
D_FF_PAD = -(-D_FF // FF_TILE) * FF_TILE


def _ffn_up_kernel(h_ref, wg_ref, wu_ref, o_ref, *, tn):
    j = pl.program_id(1)
    h = h_ref[...]
    g = _dot(h, wg_ref[...])
    u = _dot(h, wu_ref[...])
    hid = (g * jax.nn.sigmoid(g)) * u
    col = j * tn + lax.broadcasted_iota(jnp.int32, hid.shape, 1)
    o_ref[...] = jnp.where(col < D_FF, hid, 0.0).astype(o_ref.dtype)


def _ffn_up(h_bf, w_gate_bf, w_up_bf):
    m, d = h_bf.shape
    tm, tn = 1024, FF_TILE
    kern = functools.partial(_ffn_up_kernel, tn=tn)
    limit = _vmem_limit([((tm, d), BF16), ((d, tn), BF16), ((d, tn), BF16), ((tm, tn), BF16)],
                        [((tm, tn), F32)] * 3)
    return pl.pallas_call(
        kern,
        grid=(m // tm, D_FF_PAD // tn),
        in_specs=[pl.BlockSpec((tm, d), lambda i, j: (i, 0)),
                  pl.BlockSpec((d, tn), lambda i, j: (0, j)),
                  pl.BlockSpec((d, tn), lambda i, j: (0, j))],
        out_specs=pl.BlockSpec((tm, tn), lambda i, j: (i, j)),
        out_shape=jax.ShapeDtypeStruct((m, D_FF_PAD), BF16),
        compiler_params=pltpu.CompilerParams(dimension_semantics=("parallel", "arbitrary"), vmem_limit_bytes=limit),
        name="ffn_up",
    )(h_bf, w_gate_bf, w_up_bf)


DOWN_COL_CHUNK = 1024


def _ffn_down_ln_kernel(hid_ref, w_ref, h_ref, g_ref, b_ref, o_ref, *, nk):
    k = pl.program_id(1)

    @pl.when(k == 0)
    def _():
        o_ref[...] = DEEPNORM_ALPHA * h_ref[...]

    hid = hid_ref[...]
    for n0 in range(0, o_ref.shape[1], DOWN_COL_CHUNK):
        o_ref[:, n0:n0 + DOWN_COL_CHUNK] += _dot(hid, w_ref[:, n0:n0 + DOWN_COL_CHUNK])

    @pl.when(k == nk - 1)
    def _():
        _layer_norm_block(o_ref, g_ref, b_ref, (o_ref,))


def _ffn_down_ln(hid, w_down_bf, h, ln_g, ln_b):
    m, d = h.shape
    tm, tk = 512, 1024
    nk = D_FF_PAD // tk
    kern = functools.partial(_ffn_down_ln_kernel, nk=nk)
    limit = _vmem_limit([((tm, tk), BF16), ((tk, d), BF16), ((tm, d), F32), ((tm, d), F32)],
                        [((tm, DOWN_COL_CHUNK), F32)])
    return pl.pallas_call(
        kern,
        grid=(m // tm, nk),
        in_specs=[pl.BlockSpec((tm, tk), lambda i, k: (i, k)),
                  pl.BlockSpec((tk, d), lambda i, k: (k, 0)),
                  pl.BlockSpec((tm, d), lambda i, k: (i, 0)),
                  pl.BlockSpec((1, d), lambda i, k: (0, 0)),
                  pl.BlockSpec((1, d), lambda i, k: (0, 0))],
        out_specs=pl.BlockSpec((tm, d), lambda i, k: (i, 0)),
        out_shape=jax.ShapeDtypeStruct((m, d), F32),
        compiler_params=pltpu.CompilerParams(dimension_semantics=("parallel", "arbitrary"), vmem_limit_bytes=limit),
        name="ffn_down_ln",
    )(hid, w_down_bf, h, ln_g.reshape(1, d), ln_b.reshape(1, d))


def kernel(x, positions, w_in, b_glu, conv_w, conv_b, conv_ln_g, conv_ln_b, lam_q1, lam_k1, lam_q2, lam_k2,
           subln_g, w_o, ln1_g, ln1_b, w_gate, w_up, w_down, ln2_g, ln2_b):
    batch, seq, d = x.shape
    m = batch * seq
    cos_t, sin_t = _rope_tables(positions, m)
    x2 = x.reshape(m, d)
    for l in range(DEPTH):
        lambda_init = 0.8 - 0.6 * math.exp(-0.3 * l)
        w_in_bf = w_in[l].astype(BF16)
        w_o_bf = w_o[l].astype(BF16)
        w_gate_bf = w_gate[l].astype(BF16)
        w_up_bf = w_up[l].astype(BF16)
        w_down_bf = jnp.pad(w_down[l].astype(BF16), ((0, D_FF_PAD - D_FF), (0, 0)))
        x_bf = x2.astype(BF16)

        qkv = _in_proj_qkv(x_bf, w_in_bf, cos_t, sin_t)
        glu = _in_proj_glu(x_bf, w_in_bf, b_glu[l])
        conv = _conv_ln_swish(glu.reshape(batch, seq, CONV_WIDTH), conv_w[l], conv_b[l], conv_ln_g[l], conv_ln_b[l])
        attn = _diff_attention(qkv, lam_q1[l], lam_k1[l], lam_q2[l], lam_k2[l], subln_g[l], batch, seq, lambda_init)
        h, h_bf = _out_proj_ln(attn, conv.reshape(m, CONV_WIDTH), w_o_bf, x2, ln1_g[l], ln1_b[l])
        hid = _ffn_up(h_bf, w_gate_bf, w_up_bf)
        x2 = _ffn_down_ln(hid, w_down_bf, h, ln2_g[l], ln2_b[l])
    return x2.reshape(batch, seq, d)
```

```python
import functools
import math

import jax
import jax.numpy as jnp
from jax import lax
from jax.experimental import pallas as pl
from jax.experimental.pallas import tpu as pltpu

D_MODEL = 4096
DEPTH = 1
ATTN_WIDTH = D_MODEL // 2
CONV_WIDTH = D_MODEL - ATTN_WIDTH
HEAD_DIM = 128
V_HEAD_DIM = 2 * HEAD_DIM
N_DIFF_HEADS = ATTN_WIDTH // V_HEAD_DIM
CONV_TAPS = 31
ROPE_THETA = 500000.0
ROPE_DIM = HEAD_DIM // 4
ROPE_HALF = ROPE_DIM // 2
D_FF = -(-8 * D_MODEL // (3 * 256)) * 256
Q_COLS = N_DIFF_HEADS * 2 * HEAD_DIM
K_COLS = Q_COLS
V_COLS = N_DIFF_HEADS * V_HEAD_DIM
QKV_COLS = Q_COLS + K_COLS + V_COLS
GLU_COLS = 2 * CONV_WIDTH
DEEPNORM_ALPHA = (2 * DEPTH) ** 0.25
LN_EPS = 1e-5

V7X_LANES = 128
V7X_SUBLANES = 8
V7X_VMEM_BYTES = 64 * 1024 * 1024
V7X_VMEM_REQUEST_CAP = 60000 * 1024

F32 = jnp.float32
BF16 = jnp.bfloat16
NEG_BIG = -1e30


def _nbytes(shape, dtype):
    return math.prod(shape) * jnp.dtype(dtype).itemsize


def _vmem_limit(pipelined, resident=()):
    need = 2 * sum(_nbytes(s, d) for s, d in pipelined) + sum(_nbytes(s, d) for s, d in resident)
    return min(max(need + need // 4, 16 * 1024 * 1024), V7X_VMEM_REQUEST_CAP)


def _dot(a, b):
    return jnp.dot(a, b, preferred_element_type=F32)


def _layer_norm_rows(v, g, b):
    mu = jnp.mean(v, axis=-1, keepdims=True)
    vc = v - mu
    var = jnp.mean(vc * vc, axis=-1, keepdims=True)
    return vc * lax.rsqrt(var + LN_EPS) * g + b


LN_CHUNK_ROWS = 64


def _layer_norm_block(src_ref, g_ref, b_ref, dst_refs):
    g = g_ref[...]
    b = b_ref[...]

    def chunk(c, carry):
        rows = pl.ds(pl.multiple_of(c * LN_CHUNK_ROWS, LN_CHUNK_ROWS), LN_CHUNK_ROWS)
        y = _layer_norm_rows(src_ref[rows, :], g, b)
        for dst in dst_refs:
            dst[rows, :] = y.astype(dst.dtype)
        return carry

    lax.fori_loop(0, src_ref.shape[0] // LN_CHUNK_ROWS, chunk, 0)


def _rope_table_kernel(pos_ref, inv_ref, cos_ref, sin_ref):
    ang = pos_ref[...].astype(F32) * inv_ref[...]
    lane = lax.broadcasted_iota(jnp.int32, ang.shape, 1)
    s = jnp.sin(ang)
    cos_ref[...] = jnp.cos(ang)
    sin_ref[...] = jnp.where(lane < ROPE_HALF, -s, s)


def _rope_tables(positions, m):
    tm = 1024
    inv_freq = ROPE_THETA ** (-jnp.arange(0, ROPE_DIM, 2, dtype=F32) / ROPE_DIM)
    inv_lanes = jnp.concatenate([inv_freq, inv_freq, jnp.zeros((V7X_LANES - ROPE_DIM,), F32)]).reshape(1, V7X_LANES)
    out = jax.ShapeDtypeStruct((m, V7X_LANES), F32)
    return pl.pallas_call(
        _rope_table_kernel,
        grid=(m // tm,),
        in_specs=[pl.BlockSpec((tm, 1), lambda i: (i, 0)),
                  pl.BlockSpec((1, V7X_LANES), lambda i: (0, 0))],
        out_specs=[pl.BlockSpec((tm, V7X_LANES), lambda i: (i, 0))] * 2,
        out_shape=[out, out],
        name="rope_tables",
    )(positions.reshape(m, 1), inv_lanes)


def _ahead_row(jj, i):
    return jnp.where(jj == 0, 0, i)


def _ahead_col(jj):
    return jnp.maximum(jj - 1, 0)


def _ahead_chunk_spec(chunk, tn, n_tiles, n_rows, col0=0):
    return pl.BlockSpec((chunk, tn), lambda jj, i: (jnp.where(jj == n_tiles, n_rows - 1, i),
                                                    col0 + jnp.minimum(jj, n_tiles - 1)))


def _ahead_convert(jj, i, n_tiles, chunk_refs, slot_refs):
    chunk = chunk_refs[0].shape[0]

    @pl.when(jj < n_tiles)
    def _():
        rows = pl.ds(pl.multiple_of(i * chunk, chunk), chunk)
        for src, dst in zip(chunk_refs, slot_refs):
            dst[jj % 2, rows, :] = src[...].astype(BF16)


def _side_spec(rows, width, n_rows, n_blocks):
    return pl.BlockSpec((rows, width), lambda jj, i: (jnp.minimum(jj * n_rows + i, n_blocks - 1), 0))


SIDE_ROWS = 64


def _qkv_kernel(x_ref, wchunk_ref, cos_ref, sin_ref, side_ref, o_ref, side_o_ref, wbf_ref, *, tn, n_tiles, n_rot_tiles):
    jj = pl.program_id(0)
    i = pl.program_id(1)
    side_o_ref[...] = side_ref[...].astype(side_o_ref.dtype)
    _ahead_convert(jj, i, n_tiles, (wchunk_ref,), (wbf_ref,))

    @pl.when(jj > 0)
    def _():
        acc = _dot(x_ref[...], wbf_ref[(jj - 1) % 2])

        @pl.when(jj - 1 < n_rot_tiles)
        def _():
            c = cos_ref[...]
            s = sin_ref[...]
            lane = lax.broadcasted_iota(jnp.int32, c.shape, 1)
            for g in range(tn // V7X_LANES):
                t = acc[:, g * V7X_LANES:(g + 1) * V7X_LANES]
                partner = jnp.where(lane < ROPE_HALF,
                                    pltpu.roll(t, V7X_LANES - ROPE_HALF, 1),
                                    pltpu.roll(t, ROPE_HALF, 1))
                o_ref[:, g * V7X_LANES:(g + 1) * V7X_LANES] = (t * c + partner * s).astype(o_ref.dtype)

        @pl.when(jj - 1 >= n_rot_tiles)
        def _():
            o_ref[...] = acc.astype(o_ref.dtype)


def _in_proj_qkv(x_bf, w_in, cos_t, sin_t, w_side):
    m, d = x_bf.shape
    tm, tn = 1024, 512
    n_rows, n_tiles = m // tm, QKV_COLS // tn
    chunk = d // n_rows
    side_blocks = w_side.shape[0] // SIDE_ROWS
    assert side_blocks <= (n_tiles + 1) * n_rows
    kern = functools.partial(_qkv_kernel, tn=tn, n_tiles=n_tiles, n_rot_tiles=(Q_COLS + K_COLS) // tn)
    row_map = lambda jj, i: (_ahead_row(jj, i), 0)
    side = _side_spec(SIDE_ROWS, w_side.shape[1], n_rows, side_blocks)
    limit = _vmem_limit([((tm, d), BF16), ((chunk, tn), F32), ((tm, V7X_LANES), F32), ((tm, V7X_LANES), F32),
                         ((SIDE_ROWS, w_side.shape[1]), F32), ((tm, tn), BF16), ((SIDE_ROWS, w_side.shape[1]), BF16)],
                        [((2, d, tn), BF16), ((tm, tn), F32)])
    return pl.pallas_call(
        kern,
        grid=(n_tiles + 1, n_rows),
        in_specs=[pl.BlockSpec((tm, d), row_map),
                  _ahead_chunk_spec(chunk, tn, n_tiles, n_rows),
                  pl.BlockSpec((tm, V7X_LANES), row_map),
                  pl.BlockSpec((tm, V7X_LANES), row_map),
                  side],
        out_specs=[pl.BlockSpec((tm, tn), lambda jj, i: (_ahead_row(jj, i), _ahead_col(jj))), side],
        out_shape=[jax.ShapeDtypeStruct((m, QKV_COLS), BF16), jax.ShapeDtypeStruct(w_side.shape, BF16)],
        scratch_shapes=[pltpu.VMEM((2, d, tn), BF16)],
        compiler_params=pltpu.CompilerParams(dimension_semantics=("arbitrary", "arbitrary"), vmem_limit_bytes=limit),
        name="in_proj_qkv",
    )(x_bf, w_in, cos_t, sin_t, w_side)


def _glu_kernel(x_ref, wa_chunk_ref, wg_chunk_ref, ba_ref, bg_ref, o_ref, wa_bf_ref, wg_bf_ref, *, n_tiles):
    jj = pl.program_id(0)
    i = pl.program_id(1)
    _ahead_convert(jj, i, n_tiles, (wa_chunk_ref, wg_chunk_ref), (wa_bf_ref, wg_bf_ref))

    @pl.when(jj > 0)
    def _():
        x = x_ref[...]
        slot = (jj - 1) % 2
        a = _dot(x, wa_bf_ref[slot]) + ba_ref[...]
        gate = _dot(x, wg_bf_ref[slot]) + bg_ref[...]
        o_ref[...] = (a * jax.nn.sigmoid(gate)).astype(o_ref.dtype)


def _in_proj_glu(x_bf, w_in, b_glu):
    m, d = x_bf.shape
    tm, tn = 1024, 512
    n_rows, n_tiles = m // tm, CONV_WIDTH // tn
    chunk = d // n_rows
    a0 = QKV_COLS // tn
    g0 = (QKV_COLS + CONV_WIDTH) // tn
    kern = functools.partial(_glu_kernel, n_tiles=n_tiles)
    limit = _vmem_limit([((tm, d), BF16), ((chunk, tn), F32), ((chunk, tn), F32), ((tm, tn), F32)],
                        [((2, d, tn), BF16)] * 2 + [((tm, tn), F32)] * 2)
    return pl.pallas_call(
        kern,
        grid=(n_tiles + 1, n_rows),
        in_specs=[pl.BlockSpec((tm, d), lambda jj, i: (_ahead_row(jj, i), 0)),
                  _ahead_chunk_spec(chunk, tn, n_tiles, n_rows, a0),
                  _ahead_chunk_spec(chunk, tn, n_tiles, n_rows, g0),
                  pl.BlockSpec((1, tn), lambda jj, i: (0, _ahead_col(jj))),
                  pl.BlockSpec((1, tn), lambda jj, i: (0, n_tiles + _ahead_col(jj)))],
        out_specs=pl.BlockSpec((tm, tn), lambda jj, i: (_ahead_row(jj, i), _ahead_col(jj))),
        out_shape=jax.ShapeDtypeStruct((m, CONV_WIDTH), F32),
        scratch_shapes=[pltpu.VMEM((2, d, tn), BF16)] * 2,
        compiler_params=pltpu.CompilerParams(dimension_semantics=("arbitrary", "arbitrary"), vmem_limit_bytes=limit),
        name="in_proj_glu",
    )(x_bf, w_in, w_in, b_glu.reshape(1, GLU_COLS), b_glu.reshape(1, GLU_COLS))


CONV_HALO = 32
CONV_ROWS = 128
CONV_BACK = V7X_SUBLANES * ((CONV_TAPS - 1) // V7X_SUBLANES)


def _conv_ln_kernel(cur_ref, prev_ref, w_ref, cb_ref, g_ref, b_ref, o_ref, ext_ref, y_ref, *, ts):
    i = pl.program_id(1)
    width = cur_ref.shape[-1]
    ext_ref[0:CONV_HALO, :] = jnp.where(i > 0, prev_ref[0], 0.0)
    ext_ref[CONV_HALO:, :] = cur_ref[0]

    def channel_group(cg, carry):
        c0 = pl.multiple_of(cg * V7X_LANES, V7X_LANES)
        lanes = pl.ds(c0, V7X_LANES)
        for r0 in range(0, ts, CONV_ROWS):
            acc = jnp.zeros((CONV_ROWS, V7X_LANES), F32)
            win = ext_ref[pl.ds(r0, CONV_ROWS + CONV_HALO), lanes]
            for r in range(V7X_SUBLANES):
                shifted = pltpu.roll(win, r, 0) if r else win
                for q in range(CONV_BACK // V7X_SUBLANES + 1):
                    u = V7X_SUBLANES * q + r
                    if u < CONV_TAPS:
                        lo = CONV_HALO - V7X_SUBLANES * q
                        acc = acc + shifted[lo:lo + CONV_ROWS, :] * w_ref[pl.ds(CONV_TAPS - 1 - u, 1), lanes]
            y_ref[pl.ds(r0, CONV_ROWS), lanes] = acc + cb_ref[:, lanes]
        return carry

    lax.fori_loop(0, width // V7X_LANES, channel_group, 0)
    z = _layer_norm_rows(y_ref[...], g_ref[...], b_ref[...])
    o_ref[0] = (z * jax.nn.sigmoid(z)).astype(o_ref.dtype)


def _conv_ln_swish(c, conv_w, conv_b, ln_g, ln_b):
    b, s, width = c.shape
    ts = 256
    per = ts // CONV_HALO
    kern = functools.partial(_conv_ln_kernel, ts=ts)
    row = lambda a: a.reshape(1, width)
    limit = _vmem_limit([((ts, width), F32), ((CONV_HALO, width), F32), ((CONV_TAPS, width), F32), ((ts, width), BF16)],
                        [((ts + CONV_HALO, width), F32), ((ts, width), F32), ((ts, width), F32)])
    return pl.pallas_call(
        kern,
        grid=(b, s // ts),
        in_specs=[pl.BlockSpec((1, ts, width), lambda bi, i: (bi, i, 0)),
                  pl.BlockSpec((1, CONV_HALO, width), lambda bi, i: (bi, jnp.maximum(i * per - 1, 0), 0)),
                  pl.BlockSpec((CONV_TAPS, width), lambda bi, i: (0, 0)),
                  pl.BlockSpec((1, width), lambda bi, i: (0, 0)),
                  pl.BlockSpec((1, width), lambda bi, i: (0, 0)),
                  pl.BlockSpec((1, width), lambda bi, i: (0, 0))],
        out_specs=pl.BlockSpec((1, ts, width), lambda bi, i: (bi, i, 0)),
        out_shape=jax.ShapeDtypeStruct((b, s, width), BF16),
        scratch_shapes=[pltpu.VMEM((ts + CONV_HALO, width), F32), pltpu.VMEM((ts, width), F32)],
        compiler_params=pltpu.CompilerParams(dimension_semantics=("parallel", "arbitrary"), vmem_limit_bytes=limit),
        name="conv_ln_swish",
    )(c, c, conv_w, row(conv_b), row(ln_g), row(ln_b))


ATTN_BLOCK = 512
SCORE_SCALE_LOG2 = HEAD_DIM ** -0.5 * math.log2(math.e)


def _dot_nt(a, b):
    return lax.dot_general(a, b, (((1,), (1,)), ((), ())), preferred_element_type=F32)


def _attn_kernel(q1_ref, q2_ref, k1_ref, k2_ref, v_ref, lq1_ref, lk1_ref, lq2_ref, lk2_ref, g_ref, o_ref, *,
                 seq, lambda_init):
    blk = ATTN_BLOCK
    lam = (jnp.exp(jnp.sum(lq1_ref[...] * lk1_ref[...], axis=-1, keepdims=True))
           - jnp.exp(jnp.sum(lq2_ref[...] * lk2_ref[...], axis=-1, keepdims=True)) + lambda_init)
    row = lax.broadcasted_iota(jnp.int32, (blk, blk), 0)
    col = lax.broadcasted_iota(jnp.int32, (blk, blk), 1)
    causal = col <= row

    def softmax_times_v(q, k_ref, r0):
        s_diag = jnp.where(causal, _dot_nt(q, k_ref[r0:r0 + blk, :]), NEG_BIG)
        m = jnp.max(s_diag, axis=-1, keepdims=True)
        if r0:
            s_past = _dot_nt(q, k_ref[0:r0, :])
            m = jnp.maximum(m, jnp.max(s_past, axis=-1, keepdims=True))
        p = jnp.exp2((s_diag - m) * SCORE_SCALE_LOG2)
        l = jnp.sum(p, axis=-1, keepdims=True)
        acc = _dot(p.astype(BF16), v_ref[r0:r0 + blk, :])
        if r0:
            p = jnp.exp2((s_past - m) * SCORE_SCALE_LOG2)
            l = l + jnp.sum(p, axis=-1, keepdims=True)
            acc = acc + _dot(p.astype(BF16), v_ref[0:r0, :])
        return acc / l

    for r0 in range(0, seq, blk):
        o = (softmax_times_v(q1_ref[r0:r0 + blk, :], k1_ref, r0)
             - lam * softmax_times_v(q2_ref[r0:r0 + blk, :], k2_ref, r0))
        y = o * lax.rsqrt(jnp.mean(o * o, axis=-1, keepdims=True) + LN_EPS)
        o_ref[r0:r0 + blk, :] = ((y * g_ref[...]) * (1.0 - lambda_init)).astype(o_ref.dtype)


def _diff_attention(qkv, lam_q1, lam_k1, lam_q2, lam_k2, subln_g, batch, seq, lambda_init):
    m = qkv.shape[0]
    kern = functools.partial(_attn_kernel, seq=seq, lambda_init=lambda_init)
    kcol = Q_COLS // HEAD_DIM
    vcol = (Q_COLS + K_COLS) // V_HEAD_DIM
    qk_spec = lambda base, comp: pl.BlockSpec((seq, HEAD_DIM), lambda b, h: (b, base + 2 * h + comp))
    vec = lambda a: a.reshape(1, -1)
    vec_spec = lambda n: pl.BlockSpec((1, n), lambda b, h: (0, 0))
    limit = _vmem_limit([((seq, HEAD_DIM), BF16)] * 4 + [((seq, V_HEAD_DIM), BF16)] * 2,
                        [((ATTN_BLOCK, seq), F32)] * 3 + [((ATTN_BLOCK, seq), BF16)] * 2)
    return pl.pallas_call(
        kern,
        grid=(batch, N_DIFF_HEADS),
        in_specs=[qk_spec(0, 0), qk_spec(0, 1), qk_spec(kcol, 0), qk_spec(kcol, 1),
                  pl.BlockSpec((seq, V_HEAD_DIM), lambda b, h: (b, vcol + h)),
                  vec_spec(HEAD_DIM), vec_spec(HEAD_DIM), vec_spec(HEAD_DIM), vec_spec(HEAD_DIM),
                  vec_spec(V_HEAD_DIM)],
        out_specs=pl.BlockSpec((seq, V_HEAD_DIM), lambda b, h: (b, h)),
        out_shape=jax.ShapeDtypeStruct((m, ATTN_WIDTH), BF16),
        compiler_params=pltpu.CompilerParams(dimension_semantics=("parallel", "parallel"), vmem_limit_bytes=limit),
        name="diff_attention",
    )(qkv, qkv, qkv, qkv, qkv, vec(lam_q1), vec(lam_k1), vec(lam_q2), vec(lam_k2), vec(subln_g))


def _oproj_ln_kernel(a_ref, c_ref, wa_ref, wc_ref, x_ref, g_ref, b_ref, h_ref, hb_ref, *, tn, nj):
    j = pl.program_id(1)
    y = _dot(a_ref[...], wa_ref[...]) + _dot(c_ref[...], wc_ref[...]) + DEEPNORM_ALPHA * x_ref[...]
    h_ref[:, pl.ds(pl.multiple_of(j * tn, tn), tn)] = y

    @pl.when(j == nj - 1)
    def _():
        _layer_norm_block(h_ref, g_ref, b_ref, (h_ref, hb_ref))


def _out_proj_ln(attn, conv, w_o_bf, x2, ln_g, ln_b):
    m, d = x2.shape
    tm, tn = 512, 512
    nj = d // tn
    kern = functools.partial(_oproj_ln_kernel, tn=tn, nj=nj)
    limit = _vmem_limit([((tm, ATTN_WIDTH), BF16), ((tm, CONV_WIDTH), BF16), ((ATTN_WIDTH, tn), BF16),
                         ((CONV_WIDTH, tn), BF16), ((tm, tn), F32), ((tm, d), F32), ((tm, d), BF16)],
                        [((tm, tn), F32)] * 2)
    return pl.pallas_call(
        kern,
        grid=(m // tm, nj),
        in_specs=[pl.BlockSpec((tm, ATTN_WIDTH), lambda i, j: (i, 0)),
                  pl.BlockSpec((tm, CONV_WIDTH), lambda i, j: (i, 0)),
                  pl.BlockSpec((ATTN_WIDTH, tn), lambda i, j: (0, j)),
                  pl.BlockSpec((CONV_WIDTH, tn), lambda i, j: (1, j)),
                  pl.BlockSpec((tm, tn), lambda i, j: (i, j)),
                  pl.BlockSpec((1, d), lambda i, j: (0, 0)),
                  pl.BlockSpec((1, d), lambda i, j: (0, 0))],
        out_specs=[pl.BlockSpec((tm, d), lambda i, j: (i, 0)),
                   pl.BlockSpec((tm, d), lambda i, j: (i, 0))],
        out_shape=[jax.ShapeDtypeStruct((m, d), F32), jax.ShapeDtypeStruct((m, d), BF16)],
        compiler_params=pltpu.CompilerParams(dimension_semantics=("parallel", "arbitrary"), vmem_limit_bytes=limit),
        name="out_proj_ln",
    )(attn, conv, w_o_bf, w_o_bf, x2, ln_g.reshape(1, d), ln_b.reshape(1, d))


FF_TILE = 512
D_FF_PAD = -(-D_FF // FF_TILE) * FF_TILE


def _ffn_up_kernel(h_ref, wg_chunk_ref, wu_chunk_ref, side_ref, o_ref, side_o_ref, wg_bf_ref, wu_bf_ref, *,
                   tn, n_tiles, n_rows, side_valid_blocks):
    jj = pl.program_id(0)
    i = pl.program_id(1)
    side_o_ref[...] = jnp.where(jj * n_rows + i < side_valid_blocks, side_ref[...], 0.0).astype(side_o_ref.dtype)
    _ahead_convert(jj, i, n_tiles, (wg_chunk_ref, wu_chunk_ref), (wg_bf_ref, wu_bf_ref))

    @pl.when(jj > 0)
    def _():
        h = h_ref[...]
        slot = (jj - 1) % 2
        g = _dot(h, wg_bf_ref[slot])
        u = _dot(h, wu_bf_ref[slot])
        hid = (g * jax.nn.sigmoid(g)) * u
        col = (jj - 1) * tn + lax.broadcasted_iota(jnp.int32, hid.shape, 1)
        o_ref[...] = jnp.where(col < D_FF, hid, 0.0).astype(o_ref.dtype)


def _ffn_up(h_bf, w_gate, w_up, w_side):
    m, d = h_bf.shape
    tm, tn = 1024, FF_TILE
    n_rows, n_tiles = m // tm, D_FF_PAD // tn
    chunk = d // n_rows
    side_valid_blocks = w_side.shape[0] // SIDE_ROWS
    side_blocks = D_FF_PAD // SIDE_ROWS
    assert side_blocks <= (n_tiles + 1) * n_rows and side_valid_blocks * SIDE_ROWS == w_side.shape[0]
    kern = functools.partial(_ffn_up_kernel, tn=tn, n_tiles=n_tiles, n_rows=n_rows, side_valid_blocks=side_valid_blocks)
    width = w_side.shape[1]
    limit = _vmem_limit([((tm, d), BF16), ((chunk, tn), F32), ((chunk, tn), F32), ((SIDE_ROWS, width), F32),
                         ((tm, tn), BF16), ((SIDE_ROWS, width), BF16)],
                        [((2, d, tn), BF16)] * 2 + [((tm, tn), F32)] * 3)
    return pl.pallas_call(
        kern,
        grid=(n_tiles + 1, n_rows),
        in_specs=[pl.BlockSpec((tm, d), lambda jj, i: (_ahead_row(jj, i), 0)),
                  _ahead_chunk_spec(chunk, tn, n_tiles, n_rows),
                  _ahead_chunk_spec(chunk, tn, n_tiles, n_rows),
                  _side_spec(SIDE_ROWS, width, n_rows, side_valid_blocks)],
        out_specs=[pl.BlockSpec((tm, tn), lambda jj, i: (_ahead_row(jj, i), _ahead_col(jj))),
                   _side_spec(SIDE_ROWS, width, n_rows, side_blocks)],
        out_shape=[jax.ShapeDtypeStruct((m, D_FF_PAD), BF16), jax.ShapeDtypeStruct((D_FF_PAD, width), BF16)],
        scratch_shapes=[pltpu.VMEM((2, d, tn), BF16)] * 2,
        compiler_params=pltpu.CompilerParams(dimension_semantics=("arbitrary", "arbitrary"), vmem_limit_bytes=limit),
        name="ffn_up",
    )(h_bf, w_gate, w_up, w_side)


DOWN_COL_CHUNK = 1024


def _ffn_down_ln_kernel(hid_ref, w_ref, h_ref, g_ref, b_ref, o_ref, *, nk):
    k = pl.program_id(1)

    @pl.when(k == 0)
    def _():
        o_ref[...] = DEEPNORM_ALPHA * h_ref[...]

    hid = hid_ref[...]
    for n0 in range(0, o_ref.shape[1], DOWN_COL_CHUNK):
        o_ref[:, n0:n0 + DOWN_COL_CHUNK] += _dot(hid, w_ref[:, n0:n0 + DOWN_COL_CHUNK])

    @pl.when(k == nk - 1)
    def _():
        _layer_norm_block(o_ref, g_ref, b_ref, (o_ref,))


def _ffn_down_ln(hid, w_down_bf, h, ln_g, ln_b):
    m, d = h.shape
    tm, tk = 512, 1024
    nk = D_FF_PAD // tk
    kern = functools.partial(_ffn_down_ln_kernel, nk=nk)
    limit = _vmem_limit([((tm, tk), BF16), ((tk, d), BF16), ((tm, d), F32), ((tm, d), F32)],
                        [((tm, DOWN_COL_CHUNK), F32)])
    return pl.pallas_call(
        kern,
        grid=(m // tm, nk),
        in_specs=[pl.BlockSpec((tm, tk), lambda i, k: (i, k)),
                  pl.BlockSpec((tk, d), lambda i, k: (k, 0)),
                  pl.BlockSpec((tm, d), lambda i, k: (i, 0)),
                  pl.BlockSpec((1, d), lambda i, k: (0, 0)),
                  pl.BlockSpec((1, d), lambda i, k: (0, 0))],
        out_specs=pl.BlockSpec((tm, d), lambda i, k: (i, 0)),
        out_shape=jax.ShapeDtypeStruct((m, d), F32),
        compiler_params=pltpu.CompilerParams(dimension_semantics=("parallel", "arbitrary"), vmem_limit_bytes=limit),
        name="ffn_down_ln",
    )(hid, w_down_bf, h, ln_g.reshape(1, d), ln_b.reshape(1, d))


def kernel(x, positions, w_in, b_glu, conv_w, conv_b, conv_ln_g, conv_ln_b, lam_q1, lam_k1, lam_q2, lam_k2,
           subln_g, w_o, ln1_g, ln1_b, w_gate, w_up, w_down, ln2_g, ln2_b):
    batch, seq, d = x.shape
    m = batch * seq
    cos_t, sin_t = _rope_tables(positions, m)
    x2 = x.reshape(m, d)
    for l in range(DEPTH):
        lambda_init = 0.8 - 0.6 * math.exp(-0.3 * l)
        x_bf = x2.astype(BF16)
        qkv, w_o_bf = _in_proj_qkv(x_bf, w_in[l], cos_t, sin_t, w_o[l])
        glu = _in_proj_glu(x_bf, w_in[l], b_glu[l])
        conv = _conv_ln_swish(glu.reshape(batch, seq, CONV_WIDTH), conv_w[l], conv_b[l], conv_ln_g[l], conv_ln_b[l])
        attn = _diff_attention(qkv, lam_q1[l], lam_k1[l], lam_q2[l], lam_k2[l], subln_g[l], batch, seq, lambda_init)
        h, h_bf = _out_proj_ln(attn, conv.reshape(m, CONV_WIDTH), w_o_bf, x2, ln1_g[l], ln1_b[l])
        hid, w_down_bf = _ffn_up(h_bf, w_gate[l], w_up[l], w_down[l])
        x2 = _ffn_down_ln(hid, w_down_bf, h, ln2_g[l], ln2_b[l])
    return x2.reshape(batch, seq, d)
```

```python
import functools
import math

import jax
import jax.numpy as jnp
from jax import lax
from jax.experimental import pallas as pl
from jax.experimental.pallas import tpu as pltpu

D_MODEL = 4096
DEPTH = 1
ATTN_WIDTH = D_MODEL // 2
CONV_WIDTH = D_MODEL - ATTN_WIDTH
HEAD_DIM = 128
V_HEAD_DIM = 2 * HEAD_DIM
N_DIFF_HEADS = ATTN_WIDTH // V_HEAD_DIM
CONV_TAPS = 31
ROPE_THETA = 500000.0
ROPE_DIM = HEAD_DIM // 4
ROPE_HALF = ROPE_DIM // 2
D_FF = -(-8 * D_MODEL // (3 * 256)) * 256
Q_COLS = N_DIFF_HEADS * 2 * HEAD_DIM
K_COLS = Q_COLS
V_COLS = N_DIFF_HEADS * V_HEAD_DIM
QKV_COLS = Q_COLS + K_COLS + V_COLS
GLU_COLS = 2 * CONV_WIDTH
DEEPNORM_ALPHA = (2 * DEPTH) ** 0.25
LN_EPS = 1e-5

V7X_LANES = 128
V7X_SUBLANES = 8
V7X_VMEM_BYTES = 64 * 1024 * 1024
V7X_VMEM_REQUEST_CAP = 60000 * 1024

F32 = jnp.float32
BF16 = jnp.bfloat16
NEG_BIG = -1e30


def _nbytes(shape, dtype):
    return math.prod(shape) * jnp.dtype(dtype).itemsize


def _vmem_limit(pipelined, resident=()):
    need = 2 * sum(_nbytes(s, d) for s, d in pipelined) + sum(_nbytes(s, d) for s, d in resident)
    return min(max(need + need // 4, 16 * 1024 * 1024), V7X_VMEM_REQUEST_CAP)


def _dot(a, b):
    return jnp.dot(a, b, preferred_element_type=F32)


def _layer_norm_rows(v, g, b):
    mu = jnp.mean(v, axis=-1, keepdims=True)
    vc = v - mu
    var = jnp.mean(vc * vc, axis=-1, keepdims=True)
    return vc * lax.rsqrt(var + LN_EPS) * g + b


LN_CHUNK_ROWS = 64


def _layer_norm_block(src_ref, g_ref, b_ref, dst_refs):
    g = g_ref[...]
    b = b_ref[...]

    def chunk(c, carry):
        rows = pl.ds(pl.multiple_of(c * LN_CHUNK_ROWS, LN_CHUNK_ROWS), LN_CHUNK_ROWS)
        y = _layer_norm_rows(src_ref[rows, :], g, b)
        for dst in dst_refs:
            dst[rows, :] = y.astype(dst.dtype)
        return carry

    lax.fori_loop(0, src_ref.shape[0] // LN_CHUNK_ROWS, chunk, 0)


def _rope_table_kernel(pos_ref, inv_ref, cos_ref, sin_ref):
    ang = pos_ref[...].astype(F32) * inv_ref[...]
    lane = lax.broadcasted_iota(jnp.int32, ang.shape, 1)
    s = jnp.sin(ang)
    cos_ref[...] = jnp.cos(ang)
    sin_ref[...] = jnp.where(lane < ROPE_HALF, -s, s)


def _rope_tables(positions, m):
    tm = 1024
    inv_freq = ROPE_THETA ** (-jnp.arange(0, ROPE_DIM, 2, dtype=F32) / ROPE_DIM)
    inv_lanes = jnp.concatenate([inv_freq, inv_freq, jnp.zeros((V7X_LANES - ROPE_DIM,), F32)]).reshape(1, V7X_LANES)
    out = jax.ShapeDtypeStruct((m, V7X_LANES), F32)
    return pl.pallas_call(
        _rope_table_kernel,
        grid=(m // tm,),
        in_specs=[pl.BlockSpec((tm, 1), lambda i: (i, 0)),
                  pl.BlockSpec((1, V7X_LANES), lambda i: (0, 0))],
        out_specs=[pl.BlockSpec((tm, V7X_LANES), lambda i: (i, 0))] * 2,
        out_shape=[out, out],
        name="rope_tables",
    )(positions.reshape(m, 1), inv_lanes)


def _ahead_row(jj, i):
    return jnp.where(jj == 0, 0, i)


def _ahead_col(jj):
    return jnp.maximum(jj - 1, 0)


def _ahead_chunk_spec(chunk, tn, n_tiles, n_rows, col0=0):
    return pl.BlockSpec((chunk, tn), lambda jj, i: (jnp.where(jj == n_tiles, n_rows - 1, i),
                                                    col0 + jnp.minimum(jj, n_tiles - 1)))


def _ahead_convert(jj, i, chunk_refs, slot_refs):
    chunk = chunk_refs[0].shape[0]
    rows = pl.ds(pl.multiple_of(i * chunk, chunk), chunk)
    for src, dst in zip(chunk_refs, slot_refs):
        dst[jj % 2, rows, :] = src[...].astype(BF16)


def _side_spec(rows, width, n_rows, n_blocks):
    return pl.BlockSpec((rows, width), lambda jj, i: (jnp.minimum(jj * n_rows + i, n_blocks - 1), 0))


SIDE_ROWS = 64


def _qkv_kernel(x_ref, wchunk_ref, cos_ref, sin_ref, side_ref, o_ref, side_o_ref, wbf_ref, *, tn, n_rot_tiles):
    jj = pl.program_id(0)
    i = pl.program_id(1)

    def convert():
        side_o_ref[...] = side_ref[...].astype(side_o_ref.dtype)
        _ahead_convert(jj, i, (wchunk_ref,), (wbf_ref,))

    pl.when(jj == 0)(convert)

    @pl.when(jj > 0)
    def _():
        convert()
        acc = _dot(x_ref[...], wbf_ref[(jj - 1) % 2])

        @pl.when(jj - 1 < n_rot_tiles)
        def _():
            c = cos_ref[...]
            s = sin_ref[...]
            lane = lax.broadcasted_iota(jnp.int32, c.shape, 1)
            for g in range(tn // V7X_LANES):
                t = acc[:, g * V7X_LANES:(g + 1) * V7X_LANES]
                partner = jnp.where(lane < ROPE_HALF,
                                    pltpu.roll(t, V7X_LANES - ROPE_HALF, 1),
                                    pltpu.roll(t, ROPE_HALF, 1))
                o_ref[:, g * V7X_LANES:(g + 1) * V7X_LANES] = (t * c + partner * s).astype(o_ref.dtype)

        @pl.when(jj - 1 >= n_rot_tiles)
        def _():
            o_ref[...] = acc.astype(o_ref.dtype)


def _in_proj_qkv(x_bf, w_in, cos_t, sin_t, w_side):
    m, d = x_bf.shape
    tm, tn = 1024, 512
    n_rows, n_tiles = m // tm, QKV_COLS // tn
    chunk = d // n_rows
    side_blocks = w_side.shape[0] // SIDE_ROWS
    assert side_blocks <= (n_tiles + 1) * n_rows
    kern = functools.partial(_qkv_kernel, tn=tn, n_rot_tiles=(Q_COLS + K_COLS) // tn)
    row_map = lambda jj, i: (_ahead_row(jj, i), 0)
    side = _side_spec(SIDE_ROWS, w_side.shape[1], n_rows, side_blocks)
    limit = _vmem_limit([((tm, d), BF16), ((chunk, tn), F32), ((tm, V7X_LANES), F32), ((tm, V7X_LANES), F32),
                         ((SIDE_ROWS, w_side.shape[1]), F32), ((tm, tn), BF16), ((SIDE_ROWS, w_side.shape[1]), BF16)],
                        [((2, d, tn), BF16), ((tm, tn), F32)])
    return pl.pallas_call(
        kern,
        grid=(n_tiles + 1, n_rows),
        in_specs=[pl.BlockSpec((tm, d), row_map),
                  _ahead_chunk_spec(chunk, tn, n_tiles, n_rows),
                  pl.BlockSpec((tm, V7X_LANES), row_map),
                  pl.BlockSpec((tm, V7X_LANES), row_map),
                  side],
        out_specs=[pl.BlockSpec((tm, tn), lambda jj, i: (_ahead_row(jj, i), _ahead_col(jj))), side],
        out_shape=[jax.ShapeDtypeStruct((m, QKV_COLS), BF16), jax.ShapeDtypeStruct(w_side.shape, BF16)],
        scratch_shapes=[pltpu.VMEM((2, d, tn), BF16)],
        compiler_params=pltpu.CompilerParams(dimension_semantics=("arbitrary", "arbitrary"), vmem_limit_bytes=limit),
        name="in_proj_qkv",
    )(x_bf, w_in, cos_t, sin_t, w_side)


def _glu_kernel(x_ref, wa_chunk_ref, wg_chunk_ref, ba_ref, bg_ref, o_ref, wa_bf_ref, wg_bf_ref):
    jj = pl.program_id(0)
    i = pl.program_id(1)
    convert = functools.partial(_ahead_convert, jj, i, (wa_chunk_ref, wg_chunk_ref), (wa_bf_ref, wg_bf_ref))
    pl.when(jj == 0)(convert)

    @pl.when(jj > 0)
    def _():
        convert()
        x = x_ref[...]
        slot = (jj - 1) % 2
        a = _dot(x, wa_bf_ref[slot]) + ba_ref[...]
        gate = _dot(x, wg_bf_ref[slot]) + bg_ref[...]
        o_ref[...] = (a * jax.nn.sigmoid(gate)).astype(o_ref.dtype)


def _in_proj_glu(x_bf, w_in, b_glu):
    m, d = x_bf.shape
    tm, tn = 1024, 512
    n_rows, n_tiles = m // tm, CONV_WIDTH // tn
    chunk = d // n_rows
    a0 = QKV_COLS // tn
    g0 = (QKV_COLS + CONV_WIDTH) // tn
    limit = _vmem_limit([((tm, d), BF16), ((chunk, tn), F32), ((chunk, tn), F32), ((tm, tn), F32)],
                        [((2, d, tn), BF16)] * 2 + [((tm, tn), F32)] * 2)
    return pl.pallas_call(
        _glu_kernel,
        grid=(n_tiles + 1, n_rows),
        in_specs=[pl.BlockSpec((tm, d), lambda jj, i: (_ahead_row(jj, i), 0)),
                  _ahead_chunk_spec(chunk, tn, n_tiles, n_rows, a0),
                  _ahead_chunk_spec(chunk, tn, n_tiles, n_rows, g0),
                  pl.BlockSpec((1, tn), lambda jj, i: (0, _ahead_col(jj))),
                  pl.BlockSpec((1, tn), lambda jj, i: (0, n_tiles + _ahead_col(jj)))],
        out_specs=pl.BlockSpec((tm, tn), lambda jj, i: (_ahead_row(jj, i), _ahead_col(jj))),
        out_shape=jax.ShapeDtypeStruct((m, CONV_WIDTH), F32),
        scratch_shapes=[pltpu.VMEM((2, d, tn), BF16)] * 2,
        compiler_params=pltpu.CompilerParams(dimension_semantics=("arbitrary", "arbitrary"), vmem_limit_bytes=limit),
        name="in_proj_glu",
    )(x_bf, w_in, w_in, b_glu.reshape(1, GLU_COLS), b_glu.reshape(1, GLU_COLS))


CONV_HALO = 32
CONV_ROWS = 128
CONV_BACK = V7X_SUBLANES * ((CONV_TAPS - 1) // V7X_SUBLANES)


def _conv_ln_kernel(cur_ref, prev_ref, w_ref, cb_ref, g_ref, b_ref, o_ref, ext_ref, y_ref, *, ts):
    i = pl.program_id(1)
    width = cur_ref.shape[-1]
    ext_ref[0:CONV_HALO, :] = jnp.where(i > 0, prev_ref[0], 0.0)
    ext_ref[CONV_HALO:, :] = cur_ref[0]

    def channel_group(cg, carry):
        c0 = pl.multiple_of(cg * V7X_LANES, V7X_LANES)
        lanes = pl.ds(c0, V7X_LANES)
        for r0 in range(0, ts, CONV_ROWS):
            acc = jnp.zeros((CONV_ROWS, V7X_LANES), F32)
            win = ext_ref[pl.ds(r0, CONV_ROWS + CONV_HALO), lanes]
            for r in range(V7X_SUBLANES):
                shifted = pltpu.roll(win, r, 0) if r else win
                for q in range(CONV_BACK // V7X_SUBLANES + 1):
                    u = V7X_SUBLANES * q + r
                    if u < CONV_TAPS:
                        lo = CONV_HALO - V7X_SUBLANES * q
                        acc = acc + shifted[lo:lo + CONV_ROWS, :] * w_ref[pl.ds(CONV_TAPS - 1 - u, 1), lanes]
            y_ref[pl.ds(r0, CONV_ROWS), lanes] = acc + cb_ref[:, lanes]
        return carry

    lax.fori_loop(0, width // V7X_LANES, channel_group, 0)
    z = _layer_norm_rows(y_ref[...], g_ref[...], b_ref[...])
    o_ref[0] = (z * jax.nn.sigmoid(z)).astype(o_ref.dtype)


def _conv_ln_swish(c, conv_w, conv_b, ln_g, ln_b):
    b, s, width = c.shape
    ts = 256
    per = ts // CONV_HALO
    kern = functools.partial(_conv_ln_kernel, ts=ts)
    row = lambda a: a.reshape(1, width)
    limit = _vmem_limit([((ts, width), F32), ((CONV_HALO, width), F32), ((CONV_TAPS, width), F32), ((ts, width), BF16)],
                        [((ts + CONV_HALO, width), F32), ((ts, width), F32), ((ts, width), F32)])
    return pl.pallas_call(
        kern,
        grid=(b, s // ts),
        in_specs=[pl.BlockSpec((1, ts, width), lambda bi, i: (bi, i, 0)),
                  pl.BlockSpec((1, CONV_HALO, width), lambda bi, i: (bi, jnp.maximum(i * per - 1, 0), 0)),
                  pl.BlockSpec((CONV_TAPS, width), lambda bi, i: (0, 0)),
                  pl.BlockSpec((1, width), lambda bi, i: (0, 0)),
                  pl.BlockSpec((1, width), lambda bi, i: (0, 0)),
                  pl.BlockSpec((1, width), lambda bi, i: (0, 0))],
        out_specs=pl.BlockSpec((1, ts, width), lambda bi, i: (bi, i, 0)),
        out_shape=jax.ShapeDtypeStruct((b, s, width), BF16),
        scratch_shapes=[pltpu.VMEM((ts + CONV_HALO, width), F32), pltpu.VMEM((ts, width), F32)],
        compiler_params=pltpu.CompilerParams(dimension_semantics=("parallel", "arbitrary"), vmem_limit_bytes=limit),
        name="conv_ln_swish",
    )(c, c, conv_w, row(conv_b), row(ln_g), row(ln_b))


ATTN_BLOCK = 512
SCORE_SCALE_LOG2 = HEAD_DIM ** -0.5 * math.log2(math.e)


def _dot_nt(a, b):
    return lax.dot_general(a, b, (((1,), (1,)), ((), ())), preferred_element_type=F32)


def _attn_kernel(q1_ref, q2_ref, k1_ref, k2_ref, v_ref, lq1_ref, lk1_ref, lq2_ref, lk2_ref, g_ref, o_ref, *,
                 seq, lambda_init):
    blk = ATTN_BLOCK
    lam = (jnp.exp(jnp.sum(lq1_ref[...] * lk1_ref[...], axis=-1, keepdims=True))
           - jnp.exp(jnp.sum(lq2_ref[...] * lk2_ref[...], axis=-1, keepdims=True)) + lambda_init)
    row = lax.broadcasted_iota(jnp.int32, (blk, blk), 0)
    col = lax.broadcasted_iota(jnp.int32, (blk, blk), 1)
    causal = col <= row

    def softmax_times_v(q, k_ref, r0):
        s_diag = jnp.where(causal, _dot_nt(q, k_ref[r0:r0 + blk, :]), NEG_BIG)
        m = jnp.max(s_diag, axis=-1, keepdims=True)
        if r0:
            s_past = _dot_nt(q, k_ref[0:r0, :])
            m = jnp.maximum(m, jnp.max(s_past, axis=-1, keepdims=True))
        p = jnp.exp2((s_diag - m) * SCORE_SCALE_LOG2)
        l = jnp.sum(p, axis=-1, keepdims=True)
        acc = _dot(p.astype(BF16), v_ref[r0:r0 + blk, :])
        if r0:
            p = jnp.exp2((s_past - m) * SCORE_SCALE_LOG2)
            l = l + jnp.sum(p, axis=-1, keepdims=True)
            acc = acc + _dot(p.astype(BF16), v_ref[0:r0, :])
        return acc / l

    for r0 in range(0, seq, blk):
        o = (softmax_times_v(q1_ref[r0:r0 + blk, :], k1_ref, r0)
             - lam * softmax_times_v(q2_ref[r0:r0 + blk, :], k2_ref, r0))
        y = o * lax.rsqrt(jnp.mean(o * o, axis=-1, keepdims=True) + LN_EPS)
        o_ref[r0:r0 + blk, :] = ((y * g_ref[...]) * (1.0 - lambda_init)).astype(o_ref.dtype)


def _diff_attention(qkv, lam_q1, lam_k1, lam_q2, lam_k2, subln_g, batch, seq, lambda_init):
    m = qkv.shape[0]
    kern = functools.partial(_attn_kernel, seq=seq, lambda_init=lambda_init)
    kcol = Q_COLS // HEAD_DIM
    vcol = (Q_COLS + K_COLS) // V_HEAD_DIM
    qk_spec = lambda base, comp: pl.BlockSpec((seq, HEAD_DIM), lambda b, h: (b, base + 2 * h + comp))
    vec = lambda a: a.reshape(1, -1)
    vec_spec = lambda n: pl.BlockSpec((1, n), lambda b, h: (0, 0))
    limit = _vmem_limit([((seq, HEAD_DIM), BF16)] * 4 + [((seq, V_HEAD_DIM), BF16)] * 2,
                        [((ATTN_BLOCK, seq), F32)] * 3 + [((ATTN_BLOCK, seq), BF16)] * 2)
    return pl.pallas_call(
        kern,
        grid=(batch, N_DIFF_HEADS),
        in_specs=[qk_spec(0, 0), qk_spec(0, 1), qk_spec(kcol, 0), qk_spec(kcol, 1),
                  pl.BlockSpec((seq, V_HEAD_DIM), lambda b, h: (b, vcol + h)),
                  vec_spec(HEAD_DIM), vec_spec(HEAD_DIM), vec_spec(HEAD_DIM), vec_spec(HEAD_DIM),
                  vec_spec(V_HEAD_DIM)],
        out_specs=pl.BlockSpec((seq, V_HEAD_DIM), lambda b, h: (b, h)),
        out_shape=jax.ShapeDtypeStruct((m, ATTN_WIDTH), BF16),
        compiler_params=pltpu.CompilerParams(dimension_semantics=("parallel", "parallel"), vmem_limit_bytes=limit),
        name="diff_attention",
    )(qkv, qkv, qkv, qkv, qkv, vec(lam_q1), vec(lam_k1), vec(lam_q2), vec(lam_k2), vec(subln_g))


def _oproj_ln_kernel(a_ref, c_ref, wa_ref, wc_ref, x_ref, g_ref, b_ref, h_ref, hb_ref, *, tn, nj):
    j = pl.program_id(1)
    y = _dot(a_ref[...], wa_ref[...]) + _dot(c_ref[...], wc_ref[...]) + DEEPNORM_ALPHA * x_ref[...]
    h_ref[:, pl.ds(pl.multiple_of(j * tn, tn), tn)] = y

    @pl.when(j == nj - 1)
    def _():
        _layer_norm_block(h_ref, g_ref, b_ref, (h_ref, hb_ref))


def _out_proj_ln(attn, conv, w_o_bf, x2, ln_g, ln_b):
    m, d = x2.shape
    tm, tn = 512, 512
    nj = d // tn
    kern = functools.partial(_oproj_ln_kernel, tn=tn, nj=nj)
    limit = _vmem_limit([((tm, ATTN_WIDTH), BF16), ((tm, CONV_WIDTH), BF16), ((ATTN_WIDTH, tn), BF16),
                         ((CONV_WIDTH, tn), BF16), ((tm, tn), F32), ((tm, d), F32), ((tm, d), BF16)],
                        [((tm, tn), F32)] * 2)
    return pl.pallas_call(
        kern,
        grid=(m // tm, nj),
        in_specs=[pl.BlockSpec((tm, ATTN_WIDTH), lambda i, j: (i, 0)),
                  pl.BlockSpec((tm, CONV_WIDTH), lambda i, j: (i, 0)),
                  pl.BlockSpec((ATTN_WIDTH, tn), lambda i, j: (0, j)),
                  pl.BlockSpec((CONV_WIDTH, tn), lambda i, j: (1, j)),
                  pl.BlockSpec((tm, tn), lambda i, j: (i, j)),
                  pl.BlockSpec((1, d), lambda i, j: (0, 0)),
                  pl.BlockSpec((1, d), lambda i, j: (0, 0))],
        out_specs=[pl.BlockSpec((tm, d), lambda i, j: (i, 0)),
                   pl.BlockSpec((tm, d), lambda i, j: (i, 0))],
        out_shape=[jax.ShapeDtypeStruct((m, d), F32), jax.ShapeDtypeStruct((m, d), BF16)],
        compiler_params=pltpu.CompilerParams(dimension_semantics=("parallel", "arbitrary"), vmem_limit_bytes=limit),
        name="out_proj_ln",
    )(attn, conv, w_o_bf, w_o_bf, x2, ln_g.reshape(1, d), ln_b.reshape(1, d))


FF_TILE = 512
D_FF_PAD = -(-D_FF // FF_TILE) * FF_TILE


def _ffn_up_kernel(h_ref, wg_chunk_ref, wu_chunk_ref, side_ref, o_ref, side_o_ref, wg_bf_ref, wu_bf_ref, *,
                   tn, n_rows, side_valid_blocks):
    jj = pl.program_id(0)
    i = pl.program_id(1)

    def convert():
        side_o_ref[...] = jnp.where(jj * n_rows + i < side_valid_blocks, side_ref[...], 0.0).astype(side_o_ref.dtype)
        _ahead_convert(jj, i, (wg_chunk_ref, wu_chunk_ref), (wg_bf_ref, wu_bf_ref))

    pl.when(jj == 0)(convert)

    @pl.when(jj > 0)
    def _():
        convert()
        h = h_ref[...]
        slot = (jj - 1) % 2
        g = _dot(h, wg_bf_ref[slot])
        u = _dot(h, wu_bf_ref[slot])
        hid = (g * jax.nn.sigmoid(g)) * u
        col = (jj - 1) * tn + lax.broadcasted_iota(jnp.int32, hid.shape, 1)
        o_ref[...] = jnp.where(col < D_FF, hid, 0.0).astype(o_ref.dtype)


def _ffn_up(h_bf, w_gate, w_up, w_side):
    m, d = h_bf.shape
    tm, tn = 1024, FF_TILE
    n_rows, n_tiles = m // tm, D_FF_PAD // tn
    chunk = d // n_rows
    side_valid_blocks = w_side.shape[0] // SIDE_ROWS
    side_blocks = D_FF_PAD // SIDE_ROWS
    assert side_blocks <= (n_tiles + 1) * n_rows and side_valid_blocks * SIDE_ROWS == w_side.shape[0]
    kern = functools.partial(_ffn_up_kernel, tn=tn, n_rows=n_rows, side_valid_blocks=side_valid_blocks)
    width = w_side.shape[1]
    limit = _vmem_limit([((tm, d), BF16), ((chunk, tn), F32), ((chunk, tn), F32), ((SIDE_ROWS, width), F32),
                         ((tm, tn), BF16), ((SIDE_ROWS, width), BF16)],
                        [((2, d, tn), BF16)] * 2 + [((tm, tn), F32)] * 3)
    return pl.pallas_call(
        kern,
        grid=(n_tiles + 1, n_rows),
        in_specs=[pl.BlockSpec((tm, d), lambda jj, i: (_ahead_row(jj, i), 0)),
                  _ahead_chunk_spec(chunk, tn, n_tiles, n_rows),
                  _ahead_chunk_spec(chunk, tn, n_tiles, n_rows),
                  _side_spec(SIDE_ROWS, width, n_rows, side_valid_blocks)],
        out_specs=[pl.BlockSpec((tm, tn), lambda jj, i: (_ahead_row(jj, i), _ahead_col(jj))),
                   _side_spec(SIDE_ROWS, width, n_rows, side_blocks)],
        out_shape=[jax.ShapeDtypeStruct((m, D_FF_PAD), BF16), jax.ShapeDtypeStruct((D_FF_PAD, width), BF16)],
        scratch_shapes=[pltpu.VMEM((2, d, tn), BF16)] * 2,
        compiler_params=pltpu.CompilerParams(dimension_semantics=("arbitrary", "arbitrary"), vmem_limit_bytes=limit),
        name="ffn_up",
    )(h_bf, w_gate, w_up, w_side)


DOWN_COL_CHUNK = 1024


DOWN_RES_COLS = 512


def _ffn_down_ln_kernel(hid_ref, w_ref, h_ref, g_ref, b_ref, o_ref, *, nk, n_res):
    k = pl.program_id(1)

    @pl.when(k == 0)
    def _():
        o_ref[...] = jnp.zeros_like(o_ref)

    hid = hid_ref[...]
    for n0 in range(0, o_ref.shape[1], DOWN_COL_CHUNK):
        o_ref[:, n0:n0 + DOWN_COL_CHUNK] += _dot(hid, w_ref[:, n0:n0 + DOWN_COL_CHUNK])

    @pl.when(k < n_res)
    def _():
        cols = pl.ds(pl.multiple_of(k * DOWN_RES_COLS, DOWN_RES_COLS), DOWN_RES_COLS)
        o_ref[:, cols] += DEEPNORM_ALPHA * h_ref[...]

    @pl.when(k == nk - 1)
    def _():
        _layer_norm_block(o_ref, g_ref, b_ref, (o_ref,))


def _ffn_down_ln(hid, w_down_bf, h, ln_g, ln_b):
    m, d = h.shape
    tm, tk = 1024, 512
    nk = D_FF_PAD // tk
    n_res = d // DOWN_RES_COLS
    assert n_res <= nk
    kern = functools.partial(_ffn_down_ln_kernel, nk=nk, n_res=n_res)
    limit = _vmem_limit([((tm, tk), BF16), ((tk, d), BF16), ((tm, DOWN_RES_COLS), F32), ((tm, d), F32)],
                        [((tm, DOWN_COL_CHUNK), F32)])
    return pl.pallas_call(
        kern,
        grid=(m // tm, nk),
        in_specs=[pl.BlockSpec((tm, tk), lambda i, k: (i, k)),
                  pl.BlockSpec((tk, d), lambda i, k: (k, 0)),
                  pl.BlockSpec((tm, DOWN_RES_COLS), lambda i, k: (i, jnp.minimum(k, n_res - 1))),
                  pl.BlockSpec((1, d), lambda i, k: (0, 0)),
                  pl.BlockSpec((1, d), lambda i, k: (0, 0))],
        out_specs=pl.BlockSpec((tm, d), lambda i, k: (i, 0)),
        out_shape=jax.ShapeDtypeStruct((m, d), F32),
        compiler_params=pltpu.CompilerParams(dimension_semantics=("parallel", "arbitrary"), vmem_limit_bytes=limit),
        name="ffn_down_ln",
    )(hid, w_down_bf, h, ln_g.reshape(1, d), ln_b.reshape(1, d))


def kernel(x, positions, w_in, b_glu, conv_w, conv_b, conv_ln_g, conv_ln_b, lam_q1, lam_k1, lam_q2, lam_k2,
           subln_g, w_o, ln1_g, ln1_b, w_gate, w_up, w_down, ln2_g, ln2_b):
    batch, seq, d = x.shape
    m = batch * seq
    cos_t, sin_t = _rope_tables(positions, m)
    x2 = x.reshape(m, d)
    for l in range(DEPTH):
        lambda_init = 0.8 - 0.6 * math.exp(-0.3 * l)
        x_bf = x2.astype(BF16)
        qkv, w_o_bf = _in_proj_qkv(x_bf, w_in[l], cos_t, sin_t, w_o[l])
        glu = _in_proj_glu(x_bf, w_in[l], b_glu[l])
        conv = _conv_ln_swish(glu.reshape(batch, seq, CONV_WIDTH), conv_w[l], conv_b[l], conv_ln_g[l], conv_ln_b[l])
        attn = _diff_attention(qkv, lam_q1[l], lam_k1[l], lam_q2[l], lam_k2[l], subln_g[l], batch, seq, lambda_init)
        h, h_bf = _out_proj_ln(attn, conv.reshape(m, CONV_WIDTH), w_o_bf, x2, ln1_g[l], ln1_b[l])
        hid, w_down_bf = _ffn_up(h_bf, w_gate[l], w_up[l], w_down[l])
        x2 = _ffn_down_ln(hid, w_down_bf, h, ln2_g[l], ln2_b[l])
    return x2.reshape(batch, seq, d)
```

```python
import functools
import math

import jax
import jax.numpy as jnp
from jax import lax
from jax.experimental import pallas as pl
from jax.experimental.pallas import tpu as pltpu

D_MODEL = 4096
DEPTH = 1
ATTN_WIDTH = D_MODEL // 2
CONV_WIDTH = D_MODEL - ATTN_WIDTH
HEAD_DIM = 128
V_HEAD_DIM = 2 * HEAD_DIM
N_DIFF_HEADS = ATTN_WIDTH // V_HEAD_DIM
CONV_TAPS = 31
ROPE_THETA = 500000.0
ROPE_DIM = HEAD_DIM // 4
ROPE_HALF = ROPE_DIM // 2
D_FF = -(-8 * D_MODEL // (3 * 256)) * 256
Q_COLS = N_DIFF_HEADS * 2 * HEAD_DIM
K_COLS = Q_COLS
V_COLS = N_DIFF_HEADS * V_HEAD_DIM
QKV_COLS = Q_COLS + K_COLS + V_COLS
GLU_COLS = 2 * CONV_WIDTH
DEEPNORM_ALPHA = (2 * DEPTH) ** 0.25
LN_EPS = 1e-5

V7X_LANES = 128
V7X_SUBLANES = 8
V7X_VMEM_BYTES = 64 * 1024 * 1024
V7X_VMEM_REQUEST_CAP = 60000 * 1024

F32 = jnp.float32
BF16 = jnp.bfloat16
NEG_BIG = -1e30


def _nbytes(shape, dtype):
    return math.prod(shape) * jnp.dtype(dtype).itemsize


def _vmem_limit(pipelined, resident=()):
    need = 2 * sum(_nbytes(s, d) for s, d in pipelined) + sum(_nbytes(s, d) for s, d in resident)
    return min(max(need + need // 4, 16 * 1024 * 1024), V7X_VMEM_REQUEST_CAP)


def _dot(a, b):
    return jnp.dot(a, b, preferred_element_type=F32)


def _layer_norm_rows(v, g, b):
    mu = jnp.mean(v, axis=-1, keepdims=True)
    vc = v - mu
    var = jnp.mean(vc * vc, axis=-1, keepdims=True)
    return vc * lax.rsqrt(var + LN_EPS) * g + b


LN_CHUNK_ROWS = 64


def _layer_norm_block(src_ref, g_ref, b_ref, dst_refs):
    g = g_ref[...]
    b = b_ref[...]

    def chunk(c, carry):
        rows = pl.ds(pl.multiple_of(c * LN_CHUNK_ROWS, LN_CHUNK_ROWS), LN_CHUNK_ROWS)
        y = _layer_norm_rows(src_ref[rows, :], g, b)
        for dst in dst_refs:
            dst[rows, :] = y.astype(dst.dtype)
        return carry

    lax.fori_loop(0, src_ref.shape[0] // LN_CHUNK_ROWS, chunk, 0, unroll=2)


def _rope_table_kernel(pos_ref, inv_ref, cos_ref, sin_ref):
    ang = pos_ref[...].astype(F32) * inv_ref[...]
    lane = lax.broadcasted_iota(jnp.int32, ang.shape, 1)
    s = jnp.sin(ang)
    cos_ref[...] = jnp.cos(ang)
    sin_ref[...] = jnp.where(lane < ROPE_HALF, -s, s)


def _rope_tables(positions, m):
    tm = 1024
    inv_freq = ROPE_THETA ** (-jnp.arange(0, ROPE_DIM, 2, dtype=F32) / ROPE_DIM)
    inv_lanes = jnp.concatenate([inv_freq, inv_freq, jnp.zeros((V7X_LANES - ROPE_DIM,), F32)]).reshape(1, V7X_LANES)
    out = jax.ShapeDtypeStruct((m, V7X_LANES), F32)
    return pl.pallas_call(
        _rope_table_kernel,
        grid=(m // tm,),
        in_specs=[pl.BlockSpec((tm, 1), lambda i: (i, 0)),
                  pl.BlockSpec((1, V7X_LANES), lambda i: (0, 0))],
        out_specs=[pl.BlockSpec((tm, V7X_LANES), lambda i: (i, 0))] * 2,
        out_shape=[out, out],
        name="rope_tables",
    )(positions.reshape(m, 1), inv_lanes)


def _ahead_row(jj, i):
    return jnp.where(jj == 0, 0, i)


def _ahead_col(jj):
    return jnp.maximum(jj - 1, 0)


def _ahead_chunk_spec(chunk, tn, n_tiles, n_rows, col0=0):
    return pl.BlockSpec((chunk, tn), lambda jj, i: (jnp.where(jj == n_tiles, n_rows - 1, i),
                                                    col0 + jnp.minimum(jj, n_tiles - 1)))


def _ahead_convert(jj, i, chunk_refs, slot_refs):
    chunk = chunk_refs[0].shape[0]
    rows = pl.ds(pl.multiple_of(i * chunk, chunk), chunk)
    for src, dst in zip(chunk_refs, slot_refs):
        dst[jj % 2, rows, :] = src[...].astype(BF16)


def _side_spec(rows, width, n_rows, n_blocks):
    return pl.BlockSpec((rows, width), lambda jj, i: (jnp.minimum(jj * n_rows + i, n_blocks - 1), 0))


SIDE_ROWS = 64


def _qkv_kernel(x_ref, wchunk_ref, cos_ref, sin_ref, side_ref, o_ref, side_o_ref, wbf_ref, *, tn, n_rot_tiles):
    jj = pl.program_id(0)
    i = pl.program_id(1)

    def convert():
        side_o_ref[...] = side_ref[...].astype(side_o_ref.dtype)
        _ahead_convert(jj, i, (wchunk_ref,), (wbf_ref,))

    pl.when(jj == 0)(convert)

    @pl.when(jj > 0)
    def _():
        convert()
        acc = _dot(x_ref[...], wbf_ref[(jj - 1) % 2])

        @pl.when(jj - 1 < n_rot_tiles)
        def _():
            c = cos_ref[...]
            s = sin_ref[...]
            lane = lax.broadcasted_iota(jnp.int32, c.shape, 1)
            for g in range(tn // V7X_LANES):
                t = acc[:, g * V7X_LANES:(g + 1) * V7X_LANES]
                partner = jnp.where(lane < ROPE_HALF,
                                    pltpu.roll(t, V7X_LANES - ROPE_HALF, 1),
                                    pltpu.roll(t, ROPE_HALF, 1))
                o_ref[:, g * V7X_LANES:(g + 1) * V7X_LANES] = (t * c + partner * s).astype(o_ref.dtype)

        @pl.when(jj - 1 >= n_rot_tiles)
        def _():
            o_ref[...] = acc.astype(o_ref.dtype)


def _in_proj_qkv(x_bf, w_in, cos_t, sin_t, w_side):
    m, d = x_bf.shape
    tm, tn = 1024, 512
    n_rows, n_tiles = m // tm, QKV_COLS // tn
    chunk = d // n_rows
    side_blocks = w_side.shape[0] // SIDE_ROWS
    assert side_blocks <= (n_tiles + 1) * n_rows
    kern = functools.partial(_qkv_kernel, tn=tn, n_rot_tiles=(Q_COLS + K_COLS) // tn)
    row_map = lambda jj, i: (_ahead_row(jj, i), 0)
    side = _side_spec(SIDE_ROWS, w_side.shape[1], n_rows, side_blocks)
    limit = _vmem_limit([((tm, d), BF16), ((chunk, tn), F32), ((tm, V7X_LANES), F32), ((tm, V7X_LANES), F32),
                         ((SIDE_ROWS, w_side.shape[1]), F32), ((tm, tn), BF16), ((SIDE_ROWS, w_side.shape[1]), BF16)],
                        [((2, d, tn), BF16), ((tm, tn), F32)])
    return pl.pallas_call(
        kern,
        grid=(n_tiles + 1, n_rows),
        in_specs=[pl.BlockSpec((tm, d), row_map),
                  _ahead_chunk_spec(chunk, tn, n_tiles, n_rows),
                  pl.BlockSpec((tm, V7X_LANES), row_map),
                  pl.BlockSpec((tm, V7X_LANES), row_map),
                  side],
        out_specs=[pl.BlockSpec((tm, tn), lambda jj, i: (_ahead_row(jj, i), _ahead_col(jj))), side],
        out_shape=[jax.ShapeDtypeStruct((m, QKV_COLS), BF16), jax.ShapeDtypeStruct(w_side.shape, BF16)],
        scratch_shapes=[pltpu.VMEM((2, d, tn), BF16)],
        compiler_params=pltpu.CompilerParams(dimension_semantics=("arbitrary", "arbitrary"), vmem_limit_bytes=limit),
        name="in_proj_qkv",
    )(x_bf, w_in, cos_t, sin_t, w_side)


def _glu_kernel(x_ref, wa_chunk_ref, wg_chunk_ref, ba_ref, bg_ref, o_ref, wa_bf_ref, wg_bf_ref):
    jj = pl.program_id(0)
    i = pl.program_id(1)
    convert = functools.partial(_ahead_convert, jj, i, (wa_chunk_ref, wg_chunk_ref), (wa_bf_ref, wg_bf_ref))
    pl.when(jj == 0)(convert)

    @pl.when(jj > 0)
    def _():
        convert()
        x = x_ref[...]
        slot = (jj - 1) % 2
        a = _dot(x, wa_bf_ref[slot]) + ba_ref[...]
        gate = _dot(x, wg_bf_ref[slot]) + bg_ref[...]
        o_ref[...] = (a * jax.nn.sigmoid(gate)).astype(o_ref.dtype)


def _in_proj_glu(x_bf, w_in, b_glu):
    m, d = x_bf.shape
    tm, tn = 1024, 512
    n_rows, n_tiles = m // tm, CONV_WIDTH // tn
    chunk = d // n_rows
    a0 = QKV_COLS // tn
    g0 = (QKV_COLS + CONV_WIDTH) // tn
    limit = _vmem_limit([((tm, d), BF16), ((chunk, tn), F32), ((chunk, tn), F32), ((tm, tn), F32)],
                        [((2, d, tn), BF16)] * 2 + [((tm, tn), F32)] * 2)
    return pl.pallas_call(
        _glu_kernel,
        grid=(n_tiles + 1, n_rows),
        in_specs=[pl.BlockSpec((tm, d), lambda jj, i: (_ahead_row(jj, i), 0)),
                  _ahead_chunk_spec(chunk, tn, n_tiles, n_rows, a0),
                  _ahead_chunk_spec(chunk, tn, n_tiles, n_rows, g0),
                  pl.BlockSpec((1, tn), lambda jj, i: (0, _ahead_col(jj))),
                  pl.BlockSpec((1, tn), lambda jj, i: (0, n_tiles + _ahead_col(jj)))],
        out_specs=pl.BlockSpec((tm, tn), lambda jj, i: (_ahead_row(jj, i), _ahead_col(jj))),
        out_shape=jax.ShapeDtypeStruct((m, CONV_WIDTH), F32),
        scratch_shapes=[pltpu.VMEM((2, d, tn), BF16)] * 2,
        compiler_params=pltpu.CompilerParams(dimension_semantics=("arbitrary", "arbitrary"), vmem_limit_bytes=limit),
        name="in_proj_glu",
    )(x_bf, w_in, w_in, b_glu.reshape(1, GLU_COLS), b_glu.reshape(1, GLU_COLS))


CONV_HALO = 32
CONV_ROWS = 128
CONV_BACK = V7X_SUBLANES * ((CONV_TAPS - 1) // V7X_SUBLANES)


def _conv_ln_kernel(cur_ref, prev_ref, w_ref, cb_ref, g_ref, b_ref, o_ref, ext_ref, y_ref, *, ts):
    i = pl.program_id(1)
    width = cur_ref.shape[-1]
    ext_ref[0:CONV_HALO, :] = jnp.where(i > 0, prev_ref[0], 0.0)
    ext_ref[CONV_HALO:, :] = cur_ref[0]

    def channel_group(cg, carry):
        c0 = pl.multiple_of(cg * V7X_LANES, V7X_LANES)
        lanes = pl.ds(c0, V7X_LANES)
        for r0 in range(0, ts, CONV_ROWS):
            acc = jnp.zeros((CONV_ROWS, V7X_LANES), F32)
            win = ext_ref[pl.ds(r0, CONV_ROWS + CONV_HALO), lanes]
            for r in range(V7X_SUBLANES):
                shifted = pltpu.roll(win, r, 0) if r else win
                for q in range(CONV_BACK // V7X_SUBLANES + 1):
                    u = V7X_SUBLANES * q + r
                    if u < CONV_TAPS:
                        lo = CONV_HALO - V7X_SUBLANES * q
                        acc = acc + shifted[lo:lo + CONV_ROWS, :] * w_ref[pl.ds(CONV_TAPS - 1 - u, 1), lanes]
            y_ref[pl.ds(r0, CONV_ROWS), lanes] = acc + cb_ref[:, lanes]
        return carry

    lax.fori_loop(0, width // V7X_LANES, channel_group, 0)
    z = _layer_norm_rows(y_ref[...], g_ref[...], b_ref[...])
    o_ref[0] = (z * jax.nn.sigmoid(z)).astype(o_ref.dtype)


def _conv_ln_swish(c, conv_w, conv_b, ln_g, ln_b):
    b, s, width = c.shape
    ts = 256
    per = ts // CONV_HALO
    kern = functools.partial(_conv_ln_kernel, ts=ts)
    row = lambda a: a.reshape(1, width)
    limit = _vmem_limit([((ts, width), F32), ((CONV_HALO, width), F32), ((CONV_TAPS, width), F32), ((ts, width), BF16)],
                        [((ts + CONV_HALO, width), F32), ((ts, width), F32), ((ts, width), F32)])
    return pl.pallas_call(
        kern,
        grid=(b, s // ts),
        in_specs=[pl.BlockSpec((1, ts, width), lambda bi, i: (bi, i, 0)),
                  pl.BlockSpec((1, CONV_HALO, width), lambda bi, i: (bi, jnp.maximum(i * per - 1, 0), 0)),
                  pl.BlockSpec((CONV_TAPS, width), lambda bi, i: (0, 0)),
                  pl.BlockSpec((1, width), lambda bi, i: (0, 0)),
                  pl.BlockSpec((1, width), lambda bi, i: (0, 0)),
                  pl.BlockSpec((1, width), lambda bi, i: (0, 0))],
        out_specs=pl.BlockSpec((1, ts, width), lambda bi, i: (bi, i, 0)),
        out_shape=jax.ShapeDtypeStruct((b, s, width), BF16),
        scratch_shapes=[pltpu.VMEM((ts + CONV_HALO, width), F32), pltpu.VMEM((ts, width), F32)],
        compiler_params=pltpu.CompilerParams(dimension_semantics=("parallel", "arbitrary"), vmem_limit_bytes=limit),
        name="conv_ln_swish",
    )(c, c, conv_w, row(conv_b), row(ln_g), row(ln_b))


ATTN_BLOCK = 512
SCORE_SCALE_LOG2 = HEAD_DIM ** -0.5 * math.log2(math.e)


def _dot_nt(a, b):
    return lax.dot_general(a, b, (((1,), (1,)), ((), ())), preferred_element_type=F32)


def _attn_kernel(q1_ref, q2_ref, k1_ref, k2_ref, v_ref, lq1_ref, lk1_ref, lq2_ref, lk2_ref, g_ref, o_ref, *,
                 seq, lambda_init):
    blk = ATTN_BLOCK
    lam = (jnp.exp(jnp.sum(lq1_ref[...] * lk1_ref[...], axis=-1, keepdims=True))
           - jnp.exp(jnp.sum(lq2_ref[...] * lk2_ref[...], axis=-1, keepdims=True)) + lambda_init)
    row = lax.broadcasted_iota(jnp.int32, (blk, blk), 0)
    col = lax.broadcasted_iota(jnp.int32, (blk, blk), 1)
    causal = col <= row

    def softmax_times_v(q, k_ref, r0):
        s_diag = jnp.where(causal, _dot_nt(q, k_ref[r0:r0 + blk, :]), NEG_BIG)
        m = jnp.max(s_diag, axis=-1, keepdims=True)
        if r0:
            s_past = _dot_nt(q, k_ref[0:r0, :])
            m = jnp.maximum(m, jnp.max(s_past, axis=-1, keepdims=True))
        p = jnp.exp2((s_diag - m) * SCORE_SCALE_LOG2)
        l = jnp.sum(p, axis=-1, keepdims=True)
        acc = _dot(p.astype(BF16), v_ref[r0:r0 + blk, :])
        if r0:
            p = jnp.exp2((s_past - m) * SCORE_SCALE_LOG2)
            l = l + jnp.sum(p, axis=-1, keepdims=True)
            acc = acc + _dot(p.astype(BF16), v_ref[0:r0, :])
        return acc / l

    for r0 in range(0, seq, blk):
        o = (softmax_times_v(q1_ref[r0:r0 + blk, :], k1_ref, r0)
             - lam * softmax_times_v(q2_ref[r0:r0 + blk, :], k2_ref, r0))
        y = o * lax.rsqrt(jnp.mean(o * o, axis=-1, keepdims=True) + LN_EPS)
        o_ref[r0:r0 + blk, :] = ((y * g_ref[...]) * (1.0 - lambda_init)).astype(o_ref.dtype)


def _diff_attention(qkv, lam_q1, lam_k1, lam_q2, lam_k2, subln_g, batch, seq, lambda_init):
    m = qkv.shape[0]
    kern = functools.partial(_attn_kernel, seq=seq, lambda_init=lambda_init)
    kcol = Q_COLS // HEAD_DIM
    vcol = (Q_COLS + K_COLS) // V_HEAD_DIM
    qk_spec = lambda base, comp: pl.BlockSpec((seq, HEAD_DIM), lambda b, h: (b, base + 2 * h + comp))
    vec = lambda a: a.reshape(1, -1)
    vec_spec = lambda n: pl.BlockSpec((1, n), lambda b, h: (0, 0))
    limit = _vmem_limit([((seq, HEAD_DIM), BF16)] * 4 + [((seq, V_HEAD_DIM), BF16)] * 2,
                        [((ATTN_BLOCK, seq), F32)] * 3 + [((ATTN_BLOCK, seq), BF16)] * 2)
    return pl.pallas_call(
        kern,
        grid=(batch, N_DIFF_HEADS),
        in_specs=[qk_spec(0, 0), qk_spec(0, 1), qk_spec(kcol, 0), qk_spec(kcol, 1),
                  pl.BlockSpec((seq, V_HEAD_DIM), lambda b, h: (b, vcol + h)),
                  vec_spec(HEAD_DIM), vec_spec(HEAD_DIM), vec_spec(HEAD_DIM), vec_spec(HEAD_DIM),
                  vec_spec(V_HEAD_DIM)],
        out_specs=pl.BlockSpec((seq, V_HEAD_DIM), lambda b, h: (b, h)),
        out_shape=jax.ShapeDtypeStruct((m, ATTN_WIDTH), BF16),
        compiler_params=pltpu.CompilerParams(dimension_semantics=("parallel", "parallel"), vmem_limit_bytes=limit),
        name="diff_attention",
    )(qkv, qkv, qkv, qkv, qkv, vec(lam_q1), vec(lam_k1), vec(lam_q2), vec(lam_k2), vec(subln_g))


def _oproj_ln_kernel(a_ref, c_ref, wa_ref, wc_ref, x_ref, g_ref, b_ref, h_ref, hb_ref, *, tn, nj):
    j = pl.program_id(1)
    y = _dot(a_ref[...], wa_ref[...]) + _dot(c_ref[...], wc_ref[...]) + DEEPNORM_ALPHA * x_ref[...]
    h_ref[:, pl.ds(pl.multiple_of(j * tn, tn), tn)] = y

    @pl.when(j == nj - 1)
    def _():
        _layer_norm_block(h_ref, g_ref, b_ref, (h_ref, hb_ref))


def _out_proj_ln(attn, conv, w_o_bf, x2, ln_g, ln_b):
    m, d = x2.shape
    tm, tn = 512, 1024
    nj = d // tn
    kern = functools.partial(_oproj_ln_kernel, tn=tn, nj=nj)
    limit = _vmem_limit([((tm, ATTN_WIDTH), BF16), ((tm, CONV_WIDTH), BF16), ((ATTN_WIDTH, tn), BF16),
                         ((CONV_WIDTH, tn), BF16), ((tm, tn), F32), ((tm, d), F32), ((tm, d), BF16)],
                        [((tm, tn), F32)] * 2)
    return pl.pallas_call(
        kern,
        grid=(m // tm, nj),
        in_specs=[pl.BlockSpec((tm, ATTN_WIDTH), lambda i, j: (i, 0)),
                  pl.BlockSpec((tm, CONV_WIDTH), lambda i, j: (i, 0)),
                  pl.BlockSpec((ATTN_WIDTH, tn), lambda i, j: (0, j)),
                  pl.BlockSpec((CONV_WIDTH, tn), lambda i, j: (1, j)),
                  pl.BlockSpec((tm, tn), lambda i, j: (i, j)),
                  pl.BlockSpec((1, d), lambda i, j: (0, 0)),
                  pl.BlockSpec((1, d), lambda i, j: (0, 0))],
        out_specs=[pl.BlockSpec((tm, d), lambda i, j: (i, 0)),
                   pl.BlockSpec((tm, d), lambda i, j: (i, 0))],
        out_shape=[jax.ShapeDtypeStruct((m, d), F32), jax.ShapeDtypeStruct((m, d), BF16)],
        compiler_params=pltpu.CompilerParams(dimension_semantics=("parallel", "arbitrary"), vmem_limit_bytes=limit),
        name="out_proj_ln",
    )(attn, conv, w_o_bf, w_o_bf, x2, ln_g.reshape(1, d), ln_b.reshape(1, d))


FF_TILE = 512
D_FF_PAD = -(-D_FF // FF_TILE) * FF_TILE


def _ffn_up_kernel(h_ref, wg_chunk_ref, wu_chunk_ref, side_ref, o_ref, side_o_ref, wg_bf_ref, wu_bf_ref, *,
                   tn, n_rows, side_valid_blocks):
    jj = pl.program_id(0)
    i = pl.program_id(1)

    def convert():
        side_o_ref[...] = jnp.where(jj * n_rows + i < side_valid_blocks, side_ref[...], 0.0).astype(side_o_ref.dtype)
        _ahead_convert(jj, i, (wg_chunk_ref, wu_chunk_ref), (wg_bf_ref, wu_bf_ref))

    pl.when(jj == 0)(convert)

    @pl.when(jj > 0)
    def _():
        convert()
        h = h_ref[...]
        slot = (jj - 1) % 2
        g = _dot(h, wg_bf_ref[slot])
        u = _dot(h, wu_bf_ref[slot])
        hid = (g * jax.nn.sigmoid(g)) * u
        col = (jj - 1) * tn + lax.broadcasted_iota(jnp.int32, hid.shape, 1)
        o_ref[...] = jnp.where(col < D_FF, hid, 0.0).astype(o_ref.dtype)


def _ffn_up(h_bf, w_gate, w_up, w_side):
    m, d = h_bf.shape
    tm, tn = 1024, FF_TILE
    n_rows, n_tiles = m // tm, D_FF_PAD // tn
    chunk = d // n_rows
    side_valid_blocks = w_side.shape[0] // SIDE_ROWS
    side_blocks = D_FF_PAD // SIDE_ROWS
    assert side_blocks <= (n_tiles + 1) * n_rows and side_valid_blocks * SIDE_ROWS == w_side.shape[0]
    kern = functools.partial(_ffn_up_kernel, tn=tn, n_rows=n_rows, side_valid_blocks=side_valid_blocks)
    width = w_side.shape[1]
    limit = _vmem_limit([((tm, d), BF16), ((chunk, tn), F32), ((chunk, tn), F32), ((SIDE_ROWS, width), F32),
                         ((tm, tn), BF16), ((SIDE_ROWS, width), BF16)],
                        [((2, d, tn), BF16)] * 2 + [((tm, tn), F32)] * 3)
    return pl.pallas_call(
        kern,
        grid=(n_tiles + 1, n_rows),
        in_specs=[pl.BlockSpec((tm, d), lambda jj, i: (_ahead_row(jj, i), 0)),
                  _ahead_chunk_spec(chunk, tn, n_tiles, n_rows),
                  _ahead_chunk_spec(chunk, tn, n_tiles, n_rows),
                  _side_spec(SIDE_ROWS, width, n_rows, side_valid_blocks)],
        out_specs=[pl.BlockSpec((tm, tn), lambda jj, i: (_ahead_row(jj, i), _ahead_col(jj))),
                   _side_spec(SIDE_ROWS, width, n_rows, side_blocks)],
        out_shape=[jax.ShapeDtypeStruct((m, D_FF_PAD), BF16), jax.ShapeDtypeStruct((D_FF_PAD, width), BF16)],
        scratch_shapes=[pltpu.VMEM((2, d, tn), BF16)] * 2,
        compiler_params=pltpu.CompilerParams(dimension_semantics=("arbitrary", "arbitrary"), vmem_limit_bytes=limit),
        name="ffn_up",
    )(h_bf, w_gate, w_up, w_side)


DOWN_COL_CHUNK = 1024


DOWN_RES_COLS = 512


def _ffn_down_ln_kernel(hid_ref, w_ref, h_ref, g_ref, b_ref, o_ref, *, nk, n_res):
    k = pl.program_id(1)

    @pl.when(k == 0)
    def _():
        o_ref[...] = jnp.zeros_like(o_ref)

    hid = hid_ref[...]
    for n0 in range(0, o_ref.shape[1], DOWN_COL_CHUNK):
        o_ref[:, n0:n0 + DOWN_COL_CHUNK] += _dot(hid, w_ref[:, n0:n0 + DOWN_COL_CHUNK])

    @pl.when(k < n_res)
    def _():
        cols = pl.ds(pl.multiple_of(k * DOWN_RES_COLS, DOWN_RES_COLS), DOWN_RES_COLS)
        o_ref[:, cols] += DEEPNORM_ALPHA * h_ref[...]

    @pl.when(k == nk - 1)
    def _():
        _layer_norm_block(o_ref, g_ref, b_ref, (o_ref,))


def _ffn_down_ln(hid, w_down_bf, h, ln_g, ln_b):
    m, d = h.shape
    tm, tk = 1024, 512
    nk = D_FF_PAD // tk
    n_res = d // DOWN_RES_COLS
    assert n_res <= nk
    kern = functools.partial(_ffn_down_ln_kernel, nk=nk, n_res=n_res)
    limit = _vmem_limit([((tm, tk), BF16), ((tk, d), BF16), ((tm, DOWN_RES_COLS), F32), ((tm, d), F32)],
                        [((tm, DOWN_COL_CHUNK), F32)])
    return pl.pallas_call(
        kern,
        grid=(m // tm, nk),
        in_specs=[pl.BlockSpec((tm, tk), lambda i, k: (i, k)),
                  pl.BlockSpec((tk, d), lambda i, k: (k, 0)),
                  pl.BlockSpec((tm, DOWN_RES_COLS), lambda i, k: (i, jnp.minimum(k, n_res - 1))),
                  pl.BlockSpec((1, d), lambda i, k: (0, 0)),
                  pl.BlockSpec((1, d), lambda i, k: (0, 0))],
        out_specs=pl.BlockSpec((tm, d), lambda i, k: (i, 0)),
        out_shape=jax.ShapeDtypeStruct((m, d), F32),
        compiler_params=pltpu.CompilerParams(dimension_semantics=("parallel", "arbitrary"), vmem_limit_bytes=limit),
        name="ffn_down_ln",
    )(hid, w_down_bf, h, ln_g.reshape(1, d), ln_b.reshape(1, d))


def kernel(x, positions, w_in, b_glu, conv_w, conv_b, conv_ln_g, conv_ln_b, lam_q1, lam_k1, lam_q2, lam_k2,
           subln_g, w_o, ln1_g, ln1_b, w_gate, w_up, w_down, ln2_g, ln2_b):
    batch, seq, d = x.shape
    m = batch * seq
    cos_t, sin_t = _rope_tables(positions, m)
    x2 = x.reshape(m, d)
    for l in range(DEPTH):
        lambda_init = 0.8 - 0.6 * math.exp(-0.3 * l)
        x_bf = x2.astype(BF16)
        qkv, w_o_bf = _in_proj_qkv(x_bf, w_in[l], cos_t, sin_t, w_o[l])
        glu = _in_proj_glu(x_bf, w_in[l], b_glu[l])
        conv = _conv_ln_swish(glu.reshape(batch, seq, CONV_WIDTH), conv_w[l], conv_b[l], conv_ln_g[l], conv_ln_b[l])
        attn = _diff_attention(qkv, lam_q1[l], lam_k1[l], lam_q2[l], lam_k2[l], subln_g[l], batch, seq, lambda_init)
        h, h_bf = _out_proj_ln(attn, conv.reshape(m, CONV_WIDTH), w_o_bf, x2, ln1_g[l], ln1_b[l])
        hid, w_down_bf = _ffn_up(h_bf, w_gate[l], w_up[l], w_down[l])
        x2 = _ffn_down_ln(hid, w_down_bf, h, ln2_g[l], ln2_b[l])
    return x2.reshape(batch, seq, d)
```

```python
import functools
import math

import jax
import jax.numpy as jnp
from jax import lax
from jax.experimental import pallas as pl
from jax.experimental.pallas import tpu as pltpu

D_MODEL = 4096
DEPTH = 1
ATTN_WIDTH = D_MODEL // 2
CONV_WIDTH = D_MODEL - ATTN_WIDTH
HEAD_DIM = 128
V_HEAD_DIM = 2 * HEAD_DIM
N_DIFF_HEADS = ATTN_WIDTH // V_HEAD_DIM
CONV_TAPS = 31
ROPE_THETA = 500000.0
ROPE_DIM = HEAD_DIM // 4
ROPE_HALF = ROPE_DIM // 2
D_FF = -(-8 * D_MODEL // (3 * 256)) * 256
Q_COLS = N_DIFF_HEADS * 2 * HEAD_DIM
K_COLS = Q_COLS
V_COLS = N_DIFF_HEADS * V_HEAD_DIM
QKV_COLS = Q_COLS + K_COLS + V_COLS
GLU_COLS = 2 * CONV_WIDTH
DEEPNORM_ALPHA = (2 * DEPTH) ** 0.25
LN_EPS = 1e-5

V7X_LANES = 128
V7X_SUBLANES = 8
V7X_VMEM_BYTES = 64 * 1024 * 1024
V7X_VMEM_REQUEST_CAP = 60000 * 1024

F32 = jnp.float32
BF16 = jnp.bfloat16
NEG_BIG = -1e30


def _nbytes(shape, dtype):
    return math.prod(shape) * jnp.dtype(dtype).itemsize


def _vmem_limit(pipelined, resident=()):
    need = 2 * sum(_nbytes(s, d) for s, d in pipelined) + sum(_nbytes(s, d) for s, d in resident)
    return min(max(need + need // 4, 16 * 1024 * 1024), V7X_VMEM_REQUEST_CAP)


def _dot(a, b):
    return jnp.dot(a, b, preferred_element_type=F32)


def _layer_norm_rows(v, g, b):
    mu = jnp.mean(v, axis=-1, keepdims=True)
    vc = v - mu
    var = jnp.mean(vc * vc, axis=-1, keepdims=True)
    return vc * lax.rsqrt(var + LN_EPS) * g + b


LN_CHUNK_ROWS = 64
LN_COL_GROUP = 256


def _layer_norm_block(src_ref, g_ref, b_ref, dst_refs):
    width = src_ref.shape[1]
    groups = [slice(c0, c0 + LN_COL_GROUP) for c0 in range(0, width, LN_COL_GROUP)]

    def row_sums(rows, term):
        acc = term(src_ref[rows, groups[0]])
        for cols in groups[1:]:
            acc = acc + term(src_ref[rows, cols])
        return jnp.sum(acc, axis=-1, keepdims=True)

    def chunk(c, carry):
        rows = pl.ds(pl.multiple_of(c * LN_CHUNK_ROWS, LN_CHUNK_ROWS), LN_CHUNK_ROWS)
        mu = row_sums(rows, lambda v: v) * (1.0 / width)
        var = row_sums(rows, lambda v: (v - mu) * (v - mu)) * (1.0 / width)
        rstd = lax.rsqrt(var + LN_EPS)
        for cols in groups:
            y = (src_ref[rows, cols] - mu) * rstd * g_ref[:, cols] + b_ref[:, cols]
            for dst in dst_refs:
                dst[rows, cols] = y.astype(dst.dtype)
        return carry

    lax.fori_loop(0, src_ref.shape[0] // LN_CHUNK_ROWS, chunk, 0)


def _rope_table_kernel(pos_ref, inv_ref, cos_ref, sin_ref):
    ang = pos_ref[...].astype(F32) * inv_ref[...]
    lane = lax.broadcasted_iota(jnp.int32, ang.shape, 1)
    s = jnp.sin(ang)
    cos_ref[...] = jnp.cos(ang)
    sin_ref[...] = jnp.where(lane < ROPE_HALF, -s, s)


def _rope_tables(positions, m):
    tm = 1024
    inv_freq = ROPE_THETA ** (-jnp.arange(0, ROPE_DIM, 2, dtype=F32) / ROPE_DIM)
    inv_lanes = jnp.concatenate([inv_freq, inv_freq, jnp.zeros((V7X_LANES - ROPE_DIM,), F32)]).reshape(1, V7X_LANES)
    out = jax.ShapeDtypeStruct((m, V7X_LANES), F32)
    return pl.pallas_call(
        _rope_table_kernel,
        grid=(m // tm,),
        in_specs=[pl.BlockSpec((tm, 1), lambda i: (i, 0)),
                  pl.BlockSpec((1, V7X_LANES), lambda i: (0, 0))],
        out_specs=[pl.BlockSpec((tm, V7X_LANES), lambda i: (i, 0))] * 2,
        out_shape=[out, out],
        name="rope_tables",
    )(positions.reshape(m, 1), inv_lanes)


def _ahead_row(jj, i):
    return jnp.where(jj == 0, 0, i)


def _ahead_col(jj):
    return jnp.maximum(jj - 1, 0)


def _ahead_chunk_spec(chunk, tn, n_tiles, n_rows, col0=0):
    return pl.BlockSpec((chunk, tn), lambda jj, i: (jnp.where(jj == n_tiles, n_rows - 1, i),
                                                    col0 + jnp.minimum(jj, n_tiles - 1)))


def _ahead_convert(jj, i, chunk_refs, slot_refs):
    chunk = chunk_refs[0].shape[0]
    rows = pl.ds(pl.multiple_of(i * chunk, chunk), chunk)
    for src, dst in zip(chunk_refs, slot_refs):
        dst[jj % 2, rows, :] = src[...].astype(BF16)


def _side_spec(rows, width, n_rows, n_blocks):
    return pl.BlockSpec((rows, width), lambda jj, i: (jnp.minimum(jj * n_rows + i, n_blocks - 1), 0))


SIDE_ROWS = 64


def _qkv_kernel(x_ref, wchunk_ref, cos_ref, sin_ref, side_ref, o_ref, side_o_ref, wbf_ref, *, tn, n_rot_tiles):
    jj = pl.program_id(0)
    i = pl.program_id(1)

    def convert():
        side_o_ref[...] = side_ref[...].astype(side_o_ref.dtype)
        _ahead_convert(jj, i, (wchunk_ref,), (wbf_ref,))

    pl.when(jj == 0)(convert)

    @pl.when(jj > 0)
    def _():
        convert()
        acc = _dot(x_ref[...], wbf_ref[(jj - 1) % 2])

        @pl.when(jj - 1 < n_rot_tiles)
        def _():
            c = cos_ref[...]
            s = sin_ref[...]
            lane = lax.broadcasted_iota(jnp.int32, c.shape, 1)
            for g in range(tn // V7X_LANES):
                t = acc[:, g * V7X_LANES:(g + 1) * V7X_LANES]
                partner = jnp.where(lane < ROPE_HALF,
                                    pltpu.roll(t, V7X_LANES - ROPE_HALF, 1),
                                    pltpu.roll(t, ROPE_HALF, 1))
                o_ref[:, g * V7X_LANES:(g + 1) * V7X_LANES] = (t * c + partner * s).astype(o_ref.dtype)

        @pl.when(jj - 1 >= n_rot_tiles)
        def _():
            o_ref[...] = acc.astype(o_ref.dtype)


def _in_proj_qkv(x_bf, w_in, cos_t, sin_t, w_side):
    m, d = x_bf.shape
    tm, tn = 1024, 512
    n_rows, n_tiles = m // tm, QKV_COLS // tn
    chunk = d // n_rows
    side_blocks = w_side.shape[0] // SIDE_ROWS
    assert side_blocks <= (n_tiles + 1) * n_rows
    kern = functools.partial(_qkv_kernel, tn=tn, n_rot_tiles=(Q_COLS + K_COLS) // tn)
    row_map = lambda jj, i: (_ahead_row(jj, i), 0)
    side = _side_spec(SIDE_ROWS, w_side.shape[1], n_rows, side_blocks)
    limit = _vmem_limit([((tm, d), BF16), ((chunk, tn), F32), ((tm, V7X_LANES), F32), ((tm, V7X_LANES), F32),
                         ((SIDE_ROWS, w_side.shape[1]), F32), ((tm, tn), BF16), ((SIDE_ROWS, w_side.shape[1]), BF16)],
                        [((2, d, tn), BF16), ((tm, tn), F32)])
    return pl.pallas_call(
        kern,
        grid=(n_tiles + 1, n_rows),
        in_specs=[pl.BlockSpec((tm, d), row_map),
                  _ahead_chunk_spec(chunk, tn, n_tiles, n_rows),
                  pl.BlockSpec((tm, V7X_LANES), row_map),
                  pl.BlockSpec((tm, V7X_LANES), row_map),
                  side],
        out_specs=[pl.BlockSpec((tm, tn), lambda jj, i: (_ahead_row(jj, i), _ahead_col(jj))), side],
        out_shape=[jax.ShapeDtypeStruct((m, QKV_COLS), BF16), jax.ShapeDtypeStruct(w_side.shape, BF16)],
        scratch_shapes=[pltpu.VMEM((2, d, tn), BF16)],
        compiler_params=pltpu.CompilerParams(dimension_semantics=("arbitrary", "arbitrary"), vmem_limit_bytes=limit),
        name="in_proj_qkv",
    )(x_bf, w_in, cos_t, sin_t, w_side)


def _glu_kernel(x_ref, wa_chunk_ref, wg_chunk_ref, ba_ref, bg_ref, o_ref, wa_bf_ref, wg_bf_ref):
    jj = pl.program_id(0)
    i = pl.program_id(1)
    convert = functools.partial(_ahead_convert, jj, i, (wa_chunk_ref, wg_chunk_ref), (wa_bf_ref, wg_bf_ref))
    pl.when(jj == 0)(convert)

    @pl.when(jj > 0)
    def _():
        convert()
        x = x_ref[...]
        slot = (jj - 1) % 2
        a = _dot(x, wa_bf_ref[slot]) + ba_ref[...]
        gate = _dot(x, wg_bf_ref[slot]) + bg_ref[...]
        o_ref[...] = (a * jax.nn.sigmoid(gate)).astype(o_ref.dtype)


def _in_proj_glu(x_bf, w_in, b_glu):
    m, d = x_bf.shape
    tm, tn = 1024, 512
    n_rows, n_tiles = m // tm, CONV_WIDTH // tn
    chunk = d // n_rows
    a0 = QKV_COLS // tn
    g0 = (QKV_COLS + CONV_WIDTH) // tn
    limit = _vmem_limit([((tm, d), BF16), ((chunk, tn), F32), ((chunk, tn), F32), ((tm, tn), F32)],
                        [((2, d, tn), BF16)] * 2 + [((tm, tn), F32)] * 2)
    return pl.pallas_call(
        _glu_kernel,
        grid=(n_tiles + 1, n_rows),
        in_specs=[pl.BlockSpec((tm, d), lambda jj, i: (_ahead_row(jj, i), 0)),
                  _ahead_chunk_spec(chunk, tn, n_tiles, n_rows, a0),
                  _ahead_chunk_spec(chunk, tn, n_tiles, n_rows, g0),
                  pl.BlockSpec((1, tn), lambda jj, i: (0, _ahead_col(jj))),
                  pl.BlockSpec((1, tn), lambda jj, i: (0, n_tiles + _ahead_col(jj)))],
        out_specs=pl.BlockSpec((tm, tn), lambda jj, i: (_ahead_row(jj, i), _ahead_col(jj))),
        out_shape=jax.ShapeDtypeStruct((m, CONV_WIDTH), F32),
        scratch_shapes=[pltpu.VMEM((2, d, tn), BF16)] * 2,
        compiler_params=pltpu.CompilerParams(dimension_semantics=("arbitrary", "arbitrary"), vmem_limit_bytes=limit),
        name="in_proj_glu",
    )(x_bf, w_in, w_in, b_glu.reshape(1, GLU_COLS), b_glu.reshape(1, GLU_COLS))


CONV_HALO = 32
CONV_ROWS = 128
CONV_BACK = V7X_SUBLANES * ((CONV_TAPS - 1) // V7X_SUBLANES)


def _conv_ln_kernel(cur_ref, prev_ref, w_ref, cb_ref, g_ref, b_ref, o_ref, ext_ref, y_ref, *, ts):
    i = pl.program_id(1)
    width = cur_ref.shape[-1]
    ext_ref[0:CONV_HALO, :] = jnp.where(i > 0, prev_ref[0], 0.0)
    ext_ref[CONV_HALO:, :] = cur_ref[0]

    def channel_group(cg, carry):
        c0 = pl.multiple_of(cg * V7X_LANES, V7X_LANES)
        lanes = pl.ds(c0, V7X_LANES)
        for r0 in range(0, ts, CONV_ROWS):
            acc = jnp.zeros((CONV_ROWS, V7X_LANES), F32)
            win = ext_ref[pl.ds(r0, CONV_ROWS + CONV_HALO), lanes]
            for r in range(V7X_SUBLANES):
                shifted = pltpu.roll(win, r, 0) if r else win
                for q in range(CONV_BACK // V7X_SUBLANES + 1):
                    u = V7X_SUBLANES * q + r
                    if u < CONV_TAPS:
                        lo = CONV_HALO - V7X_SUBLANES * q
                        acc = acc + shifted[lo:lo + CONV_ROWS, :] * w_ref[pl.ds(CONV_TAPS - 1 - u, 1), lanes]
            y_ref[pl.ds(r0, CONV_ROWS), lanes] = acc + cb_ref[:, lanes]
        return carry

    lax.fori_loop(0, width // V7X_LANES, channel_group, 0)
    z = _layer_norm_rows(y_ref[...], g_ref[...], b_ref[...])
    o_ref[0] = (z * jax.nn.sigmoid(z)).astype(o_ref.dtype)


def _conv_ln_swish(c, conv_w, conv_b, ln_g, ln_b):
    b, s, width = c.shape
    ts = 256
    per = ts // CONV_HALO
    kern = functools.partial(_conv_ln_kernel, ts=ts)
    row = lambda a: a.reshape(1, width)
    limit = _vmem_limit([((ts, width), F32), ((CONV_HALO, width), F32), ((CONV_TAPS, width), F32), ((ts, width), BF16)],
                        [((ts + CONV_HALO, width), F32), ((ts, width), F32), ((ts, width), F32)])
    return pl.pallas_call(
        kern,
        grid=(b, s // ts),
        in_specs=[pl.BlockSpec((1, ts, width), lambda bi, i: (bi, i, 0)),
                  pl.BlockSpec((1, CONV_HALO, width), lambda bi, i: (bi, jnp.maximum(i * per - 1, 0), 0)),
                  pl.BlockSpec((CONV_TAPS, width), lambda bi, i: (0, 0)),
                  pl.BlockSpec((1, width), lambda bi, i: (0, 0)),
                  pl.BlockSpec((1, width), lambda bi, i: (0, 0)),
                  pl.BlockSpec((1, width), lambda bi, i: (0, 0))],
        out_specs=pl.BlockSpec((1, ts, width), lambda bi, i: (bi, i, 0)),
        out_shape=jax.ShapeDtypeStruct((b, s, width), BF16),
        scratch_shapes=[pltpu.VMEM((ts + CONV_HALO, width), F32), pltpu.VMEM((ts, width), F32)],
        compiler_params=pltpu.CompilerParams(dimension_semantics=("parallel", "arbitrary"), vmem_limit_bytes=limit),
        name="conv_ln_swish",
    )(c, c, conv_w, row(conv_b), row(ln_g), row(ln_b))


ATTN_BLOCK = 512
SCORE_SCALE_LOG2 = HEAD_DIM ** -0.5 * math.log2(math.e)


def _dot_nt(a, b):
    return lax.dot_general(a, b, (((1,), (1,)), ((), ())), preferred_element_type=F32)


def _attn_kernel(q1_ref, q2_ref, k1_ref, k2_ref, v_ref, lq1_ref, lk1_ref, lq2_ref, lk2_ref, g_ref, o_ref, *,
                 seq, lambda_init):
    blk = ATTN_BLOCK
    lam = (jnp.exp(jnp.sum(lq1_ref[...] * lk1_ref[...], axis=-1, keepdims=True))
           - jnp.exp(jnp.sum(lq2_ref[...] * lk2_ref[...], axis=-1, keepdims=True)) + lambda_init)
    row = lax.broadcasted_iota(jnp.int32, (blk, blk), 0)
    col = lax.broadcasted_iota(jnp.int32, (blk, blk), 1)
    causal = col <= row

    def softmax_times_v(q, k_ref, r0):
        s_diag = jnp.where(causal, _dot_nt(q, k_ref[r0:r0 + blk, :]), NEG_BIG)
        m = jnp.max(s_diag, axis=-1, keepdims=True)
        if r0:
            s_past = _dot_nt(q, k_ref[0:r0, :])
            m = jnp.maximum(m, jnp.max(s_past, axis=-1, keepdims=True))
        p = jnp.exp2((s_diag - m) * SCORE_SCALE_LOG2)
        l = jnp.sum(p, axis=-1, keepdims=True)
        acc = _dot(p.astype(BF16), v_ref[r0:r0 + blk, :])
        if r0:
            p = jnp.exp2((s_past - m) * SCORE_SCALE_LOG2)
            l = l + jnp.sum(p, axis=-1, keepdims=True)
            acc = acc + _dot(p.astype(BF16), v_ref[0:r0, :])
        return acc / l

    for r0 in range(0, seq, blk):
        o = (softmax_times_v(q1_ref[r0:r0 + blk, :], k1_ref, r0)
             - lam * softmax_times_v(q2_ref[r0:r0 + blk, :], k2_ref, r0))
        y = o * lax.rsqrt(jnp.mean(o * o, axis=-1, keepdims=True) + LN_EPS)
        o_ref[r0:r0 + blk, :] = ((y * g_ref[...]) * (1.0 - lambda_init)).astype(o_ref.dtype)


def _diff_attention(qkv, lam_q1, lam_k1, lam_q2, lam_k2, subln_g, batch, seq, lambda_init):
    m = qkv.shape[0]
    kern = functools.partial(_attn_kernel, seq=seq, lambda_init=lambda_init)
    kcol = Q_COLS // HEAD_DIM
    vcol = (Q_COLS + K_COLS) // V_HEAD_DIM
    qk_spec = lambda base, comp: pl.BlockSpec((seq, HEAD_DIM), lambda b, h: (b, base + 2 * h + comp))
    vec = lambda a: a.reshape(1, -1)
    vec_spec = lambda n: pl.BlockSpec((1, n), lambda b, h: (0, 0))
    limit = _vmem_limit([((seq, HEAD_DIM), BF16)] * 4 + [((seq, V_HEAD_DIM), BF16)] * 2,
                        [((ATTN_BLOCK, seq), F32)] * 3 + [((ATTN_BLOCK, seq), BF16)] * 2)
    return pl.pallas_call(
        kern,
        grid=(batch, N_DIFF_HEADS),
        in_specs=[qk_spec(0, 0), qk_spec(0, 1), qk_spec(kcol, 0), qk_spec(kcol, 1),
                  pl.BlockSpec((seq, V_HEAD_DIM), lambda b, h: (b, vcol + h)),
                  vec_spec(HEAD_DIM), vec_spec(HEAD_DIM), vec_spec(HEAD_DIM), vec_spec(HEAD_DIM),
                  vec_spec(V_HEAD_DIM)],
        out_specs=pl.BlockSpec((seq, V_HEAD_DIM), lambda b, h: (b, h)),
        out_shape=jax.ShapeDtypeStruct((m, ATTN_WIDTH), BF16),
        compiler_params=pltpu.CompilerParams(dimension_semantics=("parallel", "parallel"), vmem_limit_bytes=limit),
        name="diff_attention",
    )(qkv, qkv, qkv, qkv, qkv, vec(lam_q1), vec(lam_k1), vec(lam_q2), vec(lam_k2), vec(subln_g))


def _oproj_ln_kernel(a_ref, c_ref, wa_ref, wc_ref, x_ref, g_ref, b_ref, h_ref, hb_ref, *, tn, nj):
    j = pl.program_id(1)
    y = _dot(a_ref[...], wa_ref[...]) + _dot(c_ref[...], wc_ref[...]) + DEEPNORM_ALPHA * x_ref[...]
    h_ref[:, pl.ds(pl.multiple_of(j * tn, tn), tn)] = y

    @pl.when(j == nj - 1)
    def _():
        _layer_norm_block(h_ref, g_ref, b_ref, (h_ref, hb_ref))


def _out_proj_ln(attn, conv, w_o_bf, x2, ln_g, ln_b):
    m, d = x2.shape
    tm, tn = 512, 1024
    nj = d // tn
    kern = functools.partial(_oproj_ln_kernel, tn=tn, nj=nj)
    limit = _vmem_limit([((tm, ATTN_WIDTH), BF16), ((tm, CONV_WIDTH), BF16), ((ATTN_WIDTH, tn), BF16),
                         ((CONV_WIDTH, tn), BF16), ((tm, tn), F32), ((tm, d), F32), ((tm, d), BF16)],
                        [((tm, tn), F32)] * 2)
    return pl.pallas_call(
        kern,
        grid=(m // tm, nj),
        in_specs=[pl.BlockSpec((tm, ATTN_WIDTH), lambda i, j: (i, 0)),
                  pl.BlockSpec((tm, CONV_WIDTH), lambda i, j: (i, 0)),
                  pl.BlockSpec((ATTN_WIDTH, tn), lambda i, j: (0, j)),
                  pl.BlockSpec((CONV_WIDTH, tn), lambda i, j: (1, j)),
                  pl.BlockSpec((tm, tn), lambda i, j: (i, j)),
                  pl.BlockSpec((1, d), lambda i, j: (0, 0)),
                  pl.BlockSpec((1, d), lambda i, j: (0, 0))],
        out_specs=[pl.BlockSpec((tm, d), lambda i, j: (i, 0)),
                   pl.BlockSpec((tm, d), lambda i, j: (i, 0))],
        out_shape=[jax.ShapeDtypeStruct((m, d), F32), jax.ShapeDtypeStruct((m, d), BF16)],
        compiler_params=pltpu.CompilerParams(dimension_semantics=("parallel", "arbitrary"), vmem_limit_bytes=limit),
        name="out_proj_ln",
    )(attn, conv, w_o_bf, w_o_bf, x2, ln_g.reshape(1, d), ln_b.reshape(1, d))


FF_TILE = 512
D_FF_PAD = -(-D_FF // FF_TILE) * FF_TILE


def _ffn_up_kernel(h_ref, wg_chunk_ref, wu_chunk_ref, side_ref, o_ref, side_o_ref, wg_bf_ref, wu_bf_ref, *,
                   tn, n_tiles, last_cols, n_rows, side_valid_blocks):
    jj = pl.program_id(0)
    i = pl.program_id(1)

    def convert():
        side_o_ref[...] = jnp.where(jj * n_rows + i < side_valid_blocks, side_ref[...], 0.0).astype(side_o_ref.dtype)
        _ahead_convert(jj, i, (wg_chunk_ref, wu_chunk_ref), (wg_bf_ref, wu_bf_ref))

    pl.when(jj == 0)(convert)

    def tile(ncols):
        convert()
        h = h_ref[...]
        slot = (jj - 1) % 2
        g = _dot(h, wg_bf_ref[slot, :, 0:ncols])
        u = _dot(h, wu_bf_ref[slot, :, 0:ncols])
        o_ref[:, 0:ncols] = ((g * jax.nn.sigmoid(g)) * u).astype(o_ref.dtype)
        if ncols < tn:
            o_ref[:, ncols:] = jnp.zeros((o_ref.shape[0], tn - ncols), o_ref.dtype)

    pl.when(jnp.logical_and(jj > 0, jj < n_tiles))(functools.partial(tile, tn))
    pl.when(jj == n_tiles)(functools.partial(tile, last_cols))


def _ffn_up(h_bf, w_gate, w_up, w_side):
    m, d = h_bf.shape
    tm, tn = 1024, FF_TILE
    n_rows, n_tiles = m // tm, D_FF_PAD // tn
    chunk = d // n_rows
    side_valid_blocks = w_side.shape[0] // SIDE_ROWS
    side_blocks = D_FF_PAD // SIDE_ROWS
    assert side_blocks <= (n_tiles + 1) * n_rows and side_valid_blocks * SIDE_ROWS == w_side.shape[0]
    kern = functools.partial(_ffn_up_kernel, tn=tn, n_tiles=n_tiles, last_cols=D_FF - (n_tiles - 1) * tn,
                             n_rows=n_rows, side_valid_blocks=side_valid_blocks)
    width = w_side.shape[1]
    limit = _vmem_limit([((tm, d), BF16), ((chunk, tn), F32), ((chunk, tn), F32), ((SIDE_ROWS, width), F32),
                         ((tm, tn), BF16), ((SIDE_ROWS, width), BF16)],
                        [((2, d, tn), BF16)] * 2 + [((tm, tn), F32)] * 3)
    return pl.pallas_call(
        kern,
        grid=(n_tiles + 1, n_rows),
        in_specs=[pl.BlockSpec((tm, d), lambda jj, i: (_ahead_row(jj, i), 0)),
                  _ahead_chunk_spec(chunk, tn, n_tiles, n_rows),
                  _ahead_chunk_spec(chunk, tn, n_tiles, n_rows),
                  _side_spec(SIDE_ROWS, width, n_rows, side_valid_blocks)],
        out_specs=[pl.BlockSpec((tm, tn), lambda jj, i: (_ahead_row(jj, i), _ahead_col(jj))),
                   _side_spec(SIDE_ROWS, width, n_rows, side_blocks)],
        out_shape=[jax.ShapeDtypeStruct((m, D_FF_PAD), BF16), jax.ShapeDtypeStruct((D_FF_PAD, width), BF16)],
        scratch_shapes=[pltpu.VMEM((2, d, tn), BF16)] * 2,
        compiler_params=pltpu.CompilerParams(dimension_semantics=("arbitrary", "arbitrary"), vmem_limit_bytes=limit),
        name="ffn_up",
    )(h_bf, w_gate, w_up, w_side)


DOWN_COL_CHUNK = 1024


DOWN_RES_COLS = 512


def _ffn_down_ln_kernel(hid_ref, w_ref, h_ref, g_ref, b_ref, o_ref, *, nk, n_res, last_rows):
    k = pl.program_id(1)

    @pl.when(k == 0)
    def _():
        o_ref[...] = jnp.zeros_like(o_ref)

    def accumulate(kk):
        hid = hid_ref[:, 0:kk]
        for n0 in range(0, o_ref.shape[1], DOWN_COL_CHUNK):
            o_ref[:, n0:n0 + DOWN_COL_CHUNK] += _dot(hid, w_ref[0:kk, n0:n0 + DOWN_COL_CHUNK])

    pl.when(k < nk - 1)(functools.partial(accumulate, hid_ref.shape[1]))
    pl.when(k == nk - 1)(functools.partial(accumulate, last_rows))

    @pl.when(k < n_res)
    def _():
        cols = pl.ds(pl.multiple_of(k * DOWN_RES_COLS, DOWN_RES_COLS), DOWN_RES_COLS)
        o_ref[:, cols] += DEEPNORM_ALPHA * h_ref[...]

    @pl.when(k == nk - 1)
    def _():
        _layer_norm_block(o_ref, g_ref, b_ref, (o_ref,))


def _ffn_down_ln(hid, w_down_bf, h, ln_g, ln_b):
    m, d = h.shape
    tm, tk = 1024, 512
    nk = D_FF_PAD // tk
    n_res = d // DOWN_RES_COLS
    assert n_res <= nk
    kern = functools.partial(_ffn_down_ln_kernel, nk=nk, n_res=n_res, last_rows=D_FF - (nk - 1) * tk)
    limit = _vmem_limit([((tm, tk), BF16), ((tk, d), BF16), ((tm, DOWN_RES_COLS), F32), ((tm, d), F32)],
                        [((tm, DOWN_COL_CHUNK), F32)])
    return pl.pallas_call(
        kern,
        grid=(m // tm, nk),
        in_specs=[pl.BlockSpec((tm, tk), lambda i, k: (i, k)),
                  pl.BlockSpec((tk, d), lambda i, k: (k, 0)),
                  pl.BlockSpec((tm, DOWN_RES_COLS), lambda i, k: (i, jnp.minimum(k, n_res - 1))),
                  pl.BlockSpec((1, d), lambda i, k: (0, 0)),
                  pl.BlockSpec((1, d), lambda i, k: (0, 0))],
        out_specs=pl.BlockSpec((tm, d), lambda i, k: (i, 0)),
        out_shape=jax.ShapeDtypeStruct((m, d), F32),
        compiler_params=pltpu.CompilerParams(dimension_semantics=("parallel", "arbitrary"), vmem_limit_bytes=limit),
        name="ffn_down_ln",
    )(hid, w_down_bf, h, ln_g.reshape(1, d), ln_b.reshape(1, d))


def kernel(x, positions, w_in, b_glu, conv_w, conv_b, conv_ln_g, conv_ln_b, lam_q1, lam_k1, lam_q2, lam_k2,
           subln_g, w_o, ln1_g, ln1_b, w_gate, w_up, w_down, ln2_g, ln2_b):
    batch, seq, d = x.shape
    m = batch * seq
    cos_t, sin_t = _rope_tables(positions, m)
    x2 = x.reshape(m, d)
    for l in range(DEPTH):
        lambda_init = 0.8 - 0.6 * math.exp(-0.3 * l)
        x_bf = x2.astype(BF16)
        qkv, w_o_bf = _in_proj_qkv(x_bf, w_in[l], cos_t, sin_t, w_o[l])
        glu = _in_proj_glu(x_bf, w_in[l], b_glu[l])
        conv = _conv_ln_swish(glu.reshape(batch, seq, CONV_WIDTH), conv_w[l], conv_b[l], conv_ln_g[l], conv_ln_b[l])
        attn = _diff_attention(qkv, lam_q1[l], lam_k1[l], lam_q2[l], lam_k2[l], subln_g[l], batch, seq, lambda_init)
        h, h_bf = _out_proj_ln(attn, conv.reshape(m, CONV_WIDTH), w_o_bf, x2, ln1_g[l], ln1_b[l])
        hid, w_down_bf = _ffn_up(h_bf, w_gate[l], w_up[l], w_down[l])
        x2 = _ffn_down_ln(hid, w_down_bf, h, ln2_g[l], ln2_b[l])
    return x2.reshape(batch, seq, d)
```

```python
import functools
import math

import jax
import jax.numpy as jnp
from jax import lax
from jax.experimental import pallas as pl
from jax.experimental.pallas import tpu as pltpu

D_MODEL = 4096
DEPTH = 1
ATTN_WIDTH = D_MODEL // 2
CONV_WIDTH = D_MODEL - ATTN_WIDTH
HEAD_DIM = 128
V_HEAD_DIM = 2 * HEAD_DIM
N_DIFF_HEADS = ATTN_WIDTH // V_HEAD_DIM
CONV_TAPS = 31
ROPE_THETA = 500000.0
ROPE_DIM = HEAD_DIM // 4
ROPE_HALF = ROPE_DIM // 2
D_FF = -(-8 * D_MODEL // (3 * 256)) * 256
Q_COLS = N_DIFF_HEADS * 2 * HEAD_DIM
K_COLS = Q_COLS
V_COLS = N_DIFF_HEADS * V_HEAD_DIM
QKV_COLS = Q_COLS + K_COLS + V_COLS
GLU_COLS = 2 * CONV_WIDTH
DEEPNORM_ALPHA = (2 * DEPTH) ** 0.25
LN_EPS = 1e-5

V7X_LANES = 128
V7X_SUBLANES = 8
V7X_VMEM_BYTES = 64 * 1024 * 1024
V7X_VMEM_REQUEST_CAP = 60000 * 1024

F32 = jnp.float32
BF16 = jnp.bfloat16
NEG_BIG = -1e30


def _nbytes(shape, dtype):
    return math.prod(shape) * jnp.dtype(dtype).itemsize


def _vmem_limit(pipelined, resident=()):
    need = 2 * sum(_nbytes(s, d) for s, d in pipelined) + sum(_nbytes(s, d) for s, d in resident)
    return min(max(need + need // 4, 16 * 1024 * 1024), V7X_VMEM_REQUEST_CAP)


def _dot(a, b):
    return jnp.dot(a, b, preferred_element_type=F32)


def _layer_norm_rows(v, g, b):
    mu = jnp.mean(v, axis=-1, keepdims=True)
    vc = v - mu
    var = jnp.mean(vc * vc, axis=-1, keepdims=True)
    return vc * lax.rsqrt(var + LN_EPS) * g + b


LN_CHUNK_ROWS = 64
LN_COL_GROUP = 256


def _layer_norm_block(src_ref, g_ref, b_ref, dst_refs):
    width = src_ref.shape[1]
    groups = [slice(c0, c0 + LN_COL_GROUP) for c0 in range(0, width, LN_COL_GROUP)]

    def row_sums(rows, term):
        acc = term(src_ref[rows, groups[0]])
        for cols in groups[1:]:
            acc = acc + term(src_ref[rows, cols])
        return jnp.sum(acc, axis=-1, keepdims=True)

    def chunk(c, carry):
        rows = pl.ds(pl.multiple_of(c * LN_CHUNK_ROWS, LN_CHUNK_ROWS), LN_CHUNK_ROWS)
        mu = row_sums(rows, lambda v: v) * (1.0 / width)
        var = row_sums(rows, lambda v: (v - mu) * (v - mu)) * (1.0 / width)
        rstd = lax.rsqrt(var + LN_EPS)
        for cols in groups:
            y = (src_ref[rows, cols] - mu) * rstd * g_ref[:, cols] + b_ref[:, cols]
            for dst in dst_refs:
                dst[rows, cols] = y.astype(dst.dtype)
        return carry

    lax.fori_loop(0, src_ref.shape[0] // LN_CHUNK_ROWS, chunk, 0)


def _rope_table_kernel(pos_ref, inv_ref, cos_ref, sin_ref):
    ang = pos_ref[...].astype(F32) * inv_ref[...]
    lane = lax.broadcasted_iota(jnp.int32, ang.shape, 1)
    s = jnp.sin(ang)
    cos_ref[...] = jnp.cos(ang)
    sin_ref[...] = jnp.where(lane < ROPE_HALF, -s, s)


def _rope_tables(positions, m):
    tm = 1024
    inv_freq = ROPE_THETA ** (-jnp.arange(0, ROPE_DIM, 2, dtype=F32) / ROPE_DIM)
    inv_lanes = jnp.concatenate([inv_freq, inv_freq, jnp.zeros((V7X_LANES - ROPE_DIM,), F32)]).reshape(1, V7X_LANES)
    out = jax.ShapeDtypeStruct((m, V7X_LANES), F32)
    return pl.pallas_call(
        _rope_table_kernel,
        grid=(m // tm,),
        in_specs=[pl.BlockSpec((tm, 1), lambda i: (i, 0)),
                  pl.BlockSpec((1, V7X_LANES), lambda i: (0, 0))],
        out_specs=[pl.BlockSpec((tm, V7X_LANES), lambda i: (i, 0))] * 2,
        out_shape=[out, out],
        name="rope_tables",
    )(positions.reshape(m, 1), inv_lanes)


def _ahead_row(jj, i):
    return jnp.where(jj == 0, 0, i)


def _ahead_col(jj):
    return jnp.maximum(jj - 1, 0)


def _ahead_chunk_spec(chunk, tn, n_tiles, n_rows, col0=0):
    return pl.BlockSpec((chunk, tn), lambda jj, i: (jnp.where(jj == n_tiles, n_rows - 1, i),
                                                    col0 + jnp.minimum(jj, n_tiles - 1)))


def _ahead_convert(jj, i, chunk_refs, slot_refs):
    chunk = chunk_refs[0].shape[0]
    rows = pl.ds(pl.multiple_of(i * chunk, chunk), chunk)
    for src, dst in zip(chunk_refs, slot_refs):
        dst[jj % 2, rows, :] = src[...].astype(BF16)


def _side_spec(rows, width, n_rows, n_blocks):
    return pl.BlockSpec((rows, width), lambda jj, i: (jnp.minimum(jj * n_rows + i, n_blocks - 1), 0))


SIDE_ROWS = 64


def _qkv_kernel(x_ref, wchunk_ref, cos_ref, sin_ref, side_ref, o_ref, side_o_ref, wbf_ref, *, tn, n_rot_tiles):
    jj = pl.program_id(0)
    i = pl.program_id(1)

    def convert():
        side_o_ref[...] = side_ref[...].astype(side_o_ref.dtype)
        _ahead_convert(jj, i, (wchunk_ref,), (wbf_ref,))

    pl.when(jj == 0)(convert)

    @pl.when(jj > 0)
    def _():
        convert()
        acc = _dot(x_ref[...], wbf_ref[(jj - 1) % 2])

        @pl.when(jj - 1 < n_rot_tiles)
        def _():
            c = cos_ref[...]
            s = sin_ref[...]
            lane = lax.broadcasted_iota(jnp.int32, c.shape, 1)
            for g in range(tn // V7X_LANES):
                t = acc[:, g * V7X_LANES:(g + 1) * V7X_LANES]
                partner = jnp.where(lane < ROPE_HALF,
                                    pltpu.roll(t, V7X_LANES - ROPE_HALF, 1),
                                    pltpu.roll(t, ROPE_HALF, 1))
                o_ref[:, g * V7X_LANES:(g + 1) * V7X_LANES] = (t * c + partner * s).astype(o_ref.dtype)

        @pl.when(jj - 1 >= n_rot_tiles)
        def _():
            o_ref[...] = acc.astype(o_ref.dtype)


def _in_proj_qkv(x_bf, w_in, cos_t, sin_t, w_side):
    m, d = x_bf.shape
    tm, tn = 1024, 1024
    n_rows, n_tiles = m // tm, QKV_COLS // tn
    chunk = d // n_rows
    side_rows = 2 * SIDE_ROWS
    side_blocks = w_side.shape[0] // side_rows
    assert side_blocks <= (n_tiles + 1) * n_rows
    kern = functools.partial(_qkv_kernel, tn=tn, n_rot_tiles=(Q_COLS + K_COLS) // tn)
    row_map = lambda jj, i: (_ahead_row(jj, i), 0)
    side = _side_spec(side_rows, w_side.shape[1], n_rows, side_blocks)
    limit = _vmem_limit([((tm, d), BF16), ((chunk, tn), F32), ((tm, V7X_LANES), F32), ((tm, V7X_LANES), F32),
                         ((side_rows, w_side.shape[1]), F32), ((tm, tn), BF16), ((side_rows, w_side.shape[1]), BF16)],
                        [((2, d, tn), BF16), ((tm, tn), F32)])
    return pl.pallas_call(
        kern,
        grid=(n_tiles + 1, n_rows),
        in_specs=[pl.BlockSpec((tm, d), row_map),
                  _ahead_chunk_spec(chunk, tn, n_tiles, n_rows),
                  pl.BlockSpec((tm, V7X_LANES), row_map),
                  pl.BlockSpec((tm, V7X_LANES), row_map),
                  side],
        out_specs=[pl.BlockSpec((tm, tn), lambda jj, i: (_ahead_row(jj, i), _ahead_col(jj))), side],
        out_shape=[jax.ShapeDtypeStruct((m, QKV_COLS), BF16), jax.ShapeDtypeStruct(w_side.shape, BF16)],
        scratch_shapes=[pltpu.VMEM((2, d, tn), BF16)],
        compiler_params=pltpu.CompilerParams(dimension_semantics=("arbitrary", "arbitrary"), vmem_limit_bytes=limit),
        name="in_proj_qkv",
    )(x_bf, w_in, cos_t, sin_t, w_side)


def _glu_kernel(x_ref, wa_chunk_ref, wg_chunk_ref, ba_ref, bg_ref, o_ref, wa_bf_ref, wg_bf_ref):
    jj = pl.program_id(0)
    i = pl.program_id(1)
    convert = functools.partial(_ahead_convert, jj, i, (wa_chunk_ref, wg_chunk_ref), (wa_bf_ref, wg_bf_ref))
    pl.when(jj == 0)(convert)

    @pl.when(jj > 0)
    def _():
        convert()
        x = x_ref[...]
        slot = (jj - 1) % 2
        a = _dot(x, wa_bf_ref[slot]) + ba_ref[...]
        gate = _dot(x, wg_bf_ref[slot]) + bg_ref[...]
        o_ref[...] = (a * jax.nn.sigmoid(gate)).astype(o_ref.dtype)


def _in_proj_glu(x_bf, w_in, b_glu):
    m, d = x_bf.shape
    tm, tn = 1024, 512
    n_rows, n_tiles = m // tm, CONV_WIDTH // tn
    chunk = d // n_rows
    a0 = QKV_COLS // tn
    g0 = (QKV_COLS + CONV_WIDTH) // tn
    limit = _vmem_limit([((tm, d), BF16), ((chunk, tn), F32), ((chunk, tn), F32), ((tm, tn), F32)],
                        [((2, d, tn), BF16)] * 2 + [((tm, tn), F32)] * 2)
    return pl.pallas_call(
        _glu_kernel,
        grid=(n_tiles + 1, n_rows),
        in_specs=[pl.BlockSpec((tm, d), lambda jj, i: (_ahead_row(jj, i), 0)),
                  _ahead_chunk_spec(chunk, tn, n_tiles, n_rows, a0),
                  _ahead_chunk_spec(chunk, tn, n_tiles, n_rows, g0),
                  pl.BlockSpec((1, tn), lambda jj, i: (0, _ahead_col(jj))),
                  pl.BlockSpec((1, tn), lambda jj, i: (0, n_tiles + _ahead_col(jj)))],
        out_specs=pl.BlockSpec((tm, tn), lambda jj, i: (_ahead_row(jj, i), _ahead_col(jj))),
        out_shape=jax.ShapeDtypeStruct((m, CONV_WIDTH), F32),
        scratch_shapes=[pltpu.VMEM((2, d, tn), BF16)] * 2,
        compiler_params=pltpu.CompilerParams(dimension_semantics=("arbitrary", "arbitrary"), vmem_limit_bytes=limit),
        name="in_proj_glu",
    )(x_bf, w_in, w_in, b_glu.reshape(1, GLU_COLS), b_glu.reshape(1, GLU_COLS))


CONV_HALO = 32
CONV_ROWS = 128
CONV_BACK = V7X_SUBLANES * ((CONV_TAPS - 1) // V7X_SUBLANES)


def _conv_ln_kernel(cur_ref, prev_ref, w_ref, cb_ref, g_ref, b_ref, o_ref, ext_ref, y_ref, *, ts):
    i = pl.program_id(1)
    width = cur_ref.shape[-1]
    ext_ref[0:CONV_HALO, :] = jnp.where(i > 0, prev_ref[0], 0.0)
    ext_ref[CONV_HALO:, :] = cur_ref[0]

    def channel_group(cg, carry):
        c0 = pl.multiple_of(cg * V7X_LANES, V7X_LANES)
        lanes = pl.ds(c0, V7X_LANES)
        for r0 in range(0, ts, CONV_ROWS):
            acc = jnp.zeros((CONV_ROWS, V7X_LANES), F32)
            win = ext_ref[pl.ds(r0, CONV_ROWS + CONV_HALO), lanes]
            for r in range(V7X_SUBLANES):
                shifted = pltpu.roll(win, r, 0) if r else win
                for q in range(CONV_BACK // V7X_SUBLANES + 1):
                    u = V7X_SUBLANES * q + r
                    if u < CONV_TAPS:
                        lo = CONV_HALO - V7X_SUBLANES * q
                        acc = acc + shifted[lo:lo + CONV_ROWS, :] * w_ref[pl.ds(CONV_TAPS - 1 - u, 1), lanes]
            y_ref[pl.ds(r0, CONV_ROWS), lanes] = acc + cb_ref[:, lanes]
        return carry

    lax.fori_loop(0, width // V7X_LANES, channel_group, 0)
    z = _layer_norm_rows(y_ref[...], g_ref[...], b_ref[...])
    o_ref[0] = (z * jax.nn.sigmoid(z)).astype(o_ref.dtype)


def _conv_ln_swish(c, conv_w, conv_b, ln_g, ln_b):
    b, s, width = c.shape
    ts = 256
    per = ts // CONV_HALO
    kern = functools.partial(_conv_ln_kernel, ts=ts)
    row = lambda a: a.reshape(1, width)
    limit = _vmem_limit([((ts, width), F32), ((CONV_HALO, width), F32), ((CONV_TAPS, width), F32), ((ts, width), BF16)],
                        [((ts + CONV_HALO, width), F32), ((ts, width), F32), ((ts, width), F32)])
    return pl.pallas_call(
        kern,
        grid=(b, s // ts),
        in_specs=[pl.BlockSpec((1, ts, width), lambda bi, i: (bi, i, 0)),
                  pl.BlockSpec((1, CONV_HALO, width), lambda bi, i: (bi, jnp.maximum(i * per - 1, 0), 0)),
                  pl.BlockSpec((CONV_TAPS, width), lambda bi, i: (0, 0)),
                  pl.BlockSpec((1, width), lambda bi, i: (0, 0)),
                  pl.BlockSpec((1, width), lambda bi, i: (0, 0)),
                  pl.BlockSpec((1, width), lambda bi, i: (0, 0))],
        out_specs=pl.BlockSpec((1, ts, width), lambda bi, i: (bi, i, 0)),
        out_shape=jax.ShapeDtypeStruct((b, s, width), BF16),
        scratch_shapes=[pltpu.VMEM((ts + CONV_HALO, width), F32), pltpu.VMEM((ts, width), F32)],
        compiler_params=pltpu.CompilerParams(dimension_semantics=("parallel", "arbitrary"), vmem_limit_bytes=limit),
        name="conv_ln_swish",
    )(c, c, conv_w, row(conv_b), row(ln_g), row(ln_b))


ATTN_BLOCK = 512
SCORE_SCALE_LOG2 = HEAD_DIM ** -0.5 * math.log2(math.e)


def _dot_nt(a, b):
    return lax.dot_general(a, b, (((1,), (1,)), ((), ())), preferred_element_type=F32)


def _attn_kernel(q1_ref, q2_ref, k1_ref, k2_ref, v_ref, lq1_ref, lk1_ref, lq2_ref, lk2_ref, g_ref, o_ref, *,
                 seq, lambda_init):
    blk = ATTN_BLOCK
    lam = (jnp.exp(jnp.sum(lq1_ref[...] * lk1_ref[...], axis=-1, keepdims=True))
           - jnp.exp(jnp.sum(lq2_ref[...] * lk2_ref[...], axis=-1, keepdims=True)) + lambda_init)
    row = lax.broadcasted_iota(jnp.int32, (blk, blk), 0)
    col = lax.broadcasted_iota(jnp.int32, (blk, blk), 1)
    causal = col <= row

    def softmax_times_v(q, k_ref, r0):
        s_diag = jnp.where(causal, _dot_nt(q, k_ref[r0:r0 + blk, :]), NEG_BIG)
        m = jnp.max(s_diag, axis=-1, keepdims=True)
        if r0:
            s_past = _dot_nt(q, k_ref[0:r0, :])
            m = jnp.maximum(m, jnp.max(s_past, axis=-1, keepdims=True))
        p = jnp.exp2((s_diag - m) * SCORE_SCALE_LOG2)
        l = jnp.sum(p, axis=-1, keepdims=True)
        acc = _dot(p.astype(BF16), v_ref[r0:r0 + blk, :])
        if r0:
            p = jnp.exp2((s_past - m) * SCORE_SCALE_LOG2)
            l = l + jnp.sum(p, axis=-1, keepdims=True)
            acc = acc + _dot(p.astype(BF16), v_ref[0:r0, :])
        return acc / l

    for r0 in range(0, seq, blk):
        o = (softmax_times_v(q1_ref[r0:r0 + blk, :], k1_ref, r0)
             - lam * softmax_times_v(q2_ref[r0:r0 + blk, :], k2_ref, r0))
        y = o * lax.rsqrt(jnp.mean(o * o, axis=-1, keepdims=True) + LN_EPS)
        o_ref[r0:r0 + blk, :] = ((y * g_ref[...]) * (1.0 - lambda_init)).astype(o_ref.dtype)


def _diff_attention(qkv, lam_q1, lam_k1, lam_q2, lam_k2, subln_g, batch, seq, lambda_init):
    m = qkv.shape[0]
    kern = functools.partial(_attn_kernel, seq=seq, lambda_init=lambda_init)
    kcol = Q_COLS // HEAD_DIM
    vcol = (Q_COLS + K_COLS) // V_HEAD_DIM
    qk_spec = lambda base, comp: pl.BlockSpec((seq, HEAD_DIM), lambda b, h: (b, base + 2 * h + comp))
    vec = lambda a: a.reshape(1, -1)
    vec_spec = lambda n: pl.BlockSpec((1, n), lambda b, h: (0, 0))
    limit = _vmem_limit([((seq, HEAD_DIM), BF16)] * 4 + [((seq, V_HEAD_DIM), BF16)] * 2,
                        [((ATTN_BLOCK, seq), F32)] * 3 + [((ATTN_BLOCK, seq), BF16)] * 2)
    return pl.pallas_call(
        kern,
        grid=(batch, N_DIFF_HEADS),
        in_specs=[qk_spec(0, 0), qk_spec(0, 1), qk_spec(kcol, 0), qk_spec(kcol, 1),
                  pl.BlockSpec((seq, V_HEAD_DIM), lambda b, h: (b, vcol + h)),
                  vec_spec(HEAD_DIM), vec_spec(HEAD_DIM), vec_spec(HEAD_DIM), vec_spec(HEAD_DIM),
                  vec_spec(V_HEAD_DIM)],
        out_specs=pl.BlockSpec((seq, V_HEAD_DIM), lambda b, h: (b, h)),
        out_shape=jax.ShapeDtypeStruct((m, ATTN_WIDTH), BF16),
        compiler_params=pltpu.CompilerParams(dimension_semantics=("parallel", "parallel"), vmem_limit_bytes=limit),
        name="diff_attention",
    )(qkv, qkv, qkv, qkv, qkv, vec(lam_q1), vec(lam_k1), vec(lam_q2), vec(lam_k2), vec(subln_g))


def _oproj_ln_kernel(a_ref, c_ref, wa_ref, wc_ref, x_ref, g_ref, b_ref, h_ref, hb_ref, *, tn, nj):
    j = pl.program_id(1)
    y = _dot(a_ref[...], wa_ref[...]) + _dot(c_ref[...], wc_ref[...]) + DEEPNORM_ALPHA * x_ref[...]
    h_ref[:, pl.ds(pl.multiple_of(j * tn, tn), tn)] = y

    @pl.when(j == nj - 1)
    def _():
        _layer_norm_block(h_ref, g_ref, b_ref, (h_ref, hb_ref))


def _out_proj_ln(attn, conv, w_o_bf, x2, ln_g, ln_b):
    m, d = x2.shape
    tm, tn = 512, 1024
    nj = d // tn
    kern = functools.partial(_oproj_ln_kernel, tn=tn, nj=nj)
    limit = _vmem_limit([((tm, ATTN_WIDTH), BF16), ((tm, CONV_WIDTH), BF16), ((ATTN_WIDTH, tn), BF16),
                         ((CONV_WIDTH, tn), BF16), ((tm, tn), F32), ((tm, d), F32), ((tm, d), BF16)],
                        [((tm, tn), F32)] * 2)
    return pl.pallas_call(
        kern,
        grid=(m // tm, nj),
        in_specs=[pl.BlockSpec((tm, ATTN_WIDTH), lambda i, j: (i, 0)),
                  pl.BlockSpec((tm, CONV_WIDTH), lambda i, j: (i, 0)),
                  pl.BlockSpec((ATTN_WIDTH, tn), lambda i, j: (0, j)),
                  pl.BlockSpec((CONV_WIDTH, tn), lambda i, j: (1, j)),
                  pl.BlockSpec((tm, tn), lambda i, j: (i, j)),
                  pl.BlockSpec((1, d), lambda i, j: (0, 0)),
                  pl.BlockSpec((1, d), lambda i, j: (0, 0))],
        out_specs=[pl.BlockSpec((tm, d), lambda i, j: (i, 0)),
                   pl.BlockSpec((tm, d), lambda i, j: (i, 0))],
        out_shape=[jax.ShapeDtypeStruct((m, d), F32), jax.ShapeDtypeStruct((m, d), BF16)],
        compiler_params=pltpu.CompilerParams(dimension_semantics=("parallel", "arbitrary"), vmem_limit_bytes=limit),
        name="out_proj_ln",
    )(attn, conv, w_o_bf, w_o_bf, x2, ln_g.reshape(1, d), ln_b.reshape(1, d))


FF_TILE = 512
D_FF_PAD = -(-D_FF // FF_TILE) * FF_TILE


def _ffn_up_kernel(h_ref, wg_chunk_ref, wu_chunk_ref, side_ref, o_ref, side_o_ref, wg_bf_ref, wu_bf_ref, *,
                   tn, n_tiles, last_cols, n_rows, side_valid_blocks):
    jj = pl.program_id(0)
    i = pl.program_id(1)

    def convert():
        side_o_ref[...] = jnp.where(jj * n_rows + i < side_valid_blocks, side_ref[...], 0.0).astype(side_o_ref.dtype)
        _ahead_convert(jj, i, (wg_chunk_ref, wu_chunk_ref), (wg_bf_ref, wu_bf_ref))

    pl.when(jj == 0)(convert)

    def tile(ncols):
        convert()
        h = h_ref[...]
        slot = (jj - 1) % 2
        g = _dot(h, wg_bf_ref[slot, :, 0:ncols])
        u = _dot(h, wu_bf_ref[slot, :, 0:ncols])
        o_ref[:, 0:ncols] = ((g * jax.nn.sigmoid(g)) * u).astype(o_ref.dtype)
        if ncols < tn:
            o_ref[:, ncols:] = jnp.zeros((o_ref.shape[0], tn - ncols), o_ref.dtype)

    pl.when(jnp.logical_and(jj > 0, jj < n_tiles))(functools.partial(tile, tn))
    pl.when(jj == n_tiles)(functools.partial(tile, last_cols))


def _ffn_up(h_bf, w_gate, w_up, w_side):
    m, d = h_bf.shape
    tm, tn = 1024, FF_TILE
    n_rows, n_tiles = m // tm, D_FF_PAD // tn
    chunk = d // n_rows
    side_valid_blocks = w_side.shape[0] // SIDE_ROWS
    side_blocks = D_FF_PAD // SIDE_ROWS
    assert side_blocks <= (n_tiles + 1) * n_rows and side_valid_blocks * SIDE_ROWS == w_side.shape[0]
    kern = functools.partial(_ffn_up_kernel, tn=tn, n_tiles=n_tiles, last_cols=D_FF - (n_tiles - 1) * tn,
                             n_rows=n_rows, side_valid_blocks=side_valid_blocks)
    width = w_side.shape[1]
    limit = _vmem_limit([((tm, d), BF16), ((chunk, tn), F32), ((chunk, tn), F32), ((SIDE_ROWS, width), F32),
                         ((tm, tn), BF16), ((SIDE_ROWS, width), BF16)],
                        [((2, d, tn), BF16)] * 2 + [((tm, tn), F32)] * 3)
    return pl.pallas_call(
        kern,
        grid=(n_tiles + 1, n_rows),
        in_specs=[pl.BlockSpec((tm, d), lambda jj, i: (_ahead_row(jj, i), 0)),
                  _ahead_chunk_spec(chunk, tn, n_tiles, n_rows),
                  _ahead_chunk_spec(chunk, tn, n_tiles, n_rows),
                  _side_spec(SIDE_ROWS, width, n_rows, side_valid_blocks)],
        out_specs=[pl.BlockSpec((tm, tn), lambda jj, i: (_ahead_row(jj, i), _ahead_col(jj))),
                   _side_spec(SIDE_ROWS, width, n_rows, side_blocks)],
        out_shape=[jax.ShapeDtypeStruct((m, D_FF_PAD), BF16), jax.ShapeDtypeStruct((D_FF_PAD, width), BF16)],
        scratch_shapes=[pltpu.VMEM((2, d, tn), BF16)] * 2,
        compiler_params=pltpu.CompilerParams(dimension_semantics=("arbitrary", "arbitrary"), vmem_limit_bytes=limit),
        name="ffn_up",
    )(h_bf, w_gate, w_up, w_side)


DOWN_COL_CHUNK = 1024


DOWN_RES_COLS = 512


def _ffn_down_ln_kernel(hid_ref, w_ref, h_ref, g_ref, b_ref, o_ref, *, nk, n_res, last_rows):
    k = pl.program_id(1)

    def accumulate(kk, first):
        hid = hid_ref[:, 0:kk]
        for n0 in range(0, o_ref.shape[1], DOWN_COL_CHUNK):
            part = _dot(hid, w_ref[0:kk, n0:n0 + DOWN_COL_CHUNK])
            if first:
                o_ref[:, n0:n0 + DOWN_COL_CHUNK] = part
            else:
                o_ref[:, n0:n0 + DOWN_COL_CHUNK] += part

    pl.when(k == 0)(functools.partial(accumulate, hid_ref.shape[1], True))
    pl.when(jnp.logical_and(k > 0, k < nk - 1))(functools.partial(accumulate, hid_ref.shape[1], False))
    pl.when(k == nk - 1)(functools.partial(accumulate, last_rows, False))

    @pl.when(k < n_res)
    def _():
        cols = pl.ds(pl.multiple_of(k * DOWN_RES_COLS, DOWN_RES_COLS), DOWN_RES_COLS)
        o_ref[:, cols] += DEEPNORM_ALPHA * h_ref[...]

    @pl.when(k == nk - 1)
    def _():
        _layer_norm_block(o_ref, g_ref, b_ref, (o_ref,))


def _ffn_down_ln(hid, w_down_bf, h, ln_g, ln_b):
    m, d = h.shape
    tm, tk = 1024, 512
    nk = D_FF_PAD // tk
    n_res = d // DOWN_RES_COLS
    assert n_res <= nk
    kern = functools.partial(_ffn_down_ln_kernel, nk=nk, n_res=n_res, last_rows=D_FF - (nk - 1) * tk)
    limit = _vmem_limit([((tm, tk), BF16), ((tk, d), BF16), ((tm, DOWN_RES_COLS), F32), ((tm, d), F32)],
                        [((tm, DOWN_COL_CHUNK), F32)])
    return pl.pallas_call(
        kern,
        grid=(m // tm, nk),
        in_specs=[pl.BlockSpec((tm, tk), lambda i, k: (i, k)),
                  pl.BlockSpec((tk, d), lambda i, k: (k, 0)),
                  pl.BlockSpec((tm, DOWN_RES_COLS), lambda i, k: (i, jnp.minimum(k, n_res - 1))),
                  pl.BlockSpec((1, d), lambda i, k: (0, 0)),
                  pl.BlockSpec((1, d), lambda i, k: (0, 0))],
        out_specs=pl.BlockSpec((tm, d), lambda i, k: (i, 0)),
        out_shape=jax.ShapeDtypeStruct((m, d), F32),
        compiler_params=pltpu.CompilerParams(dimension_semantics=("parallel", "arbitrary"), vmem_limit_bytes=limit),
        name="ffn_down_ln",
    )(hid, w_down_bf, h, ln_g.reshape(1, d), ln_b.reshape(1, d))


def kernel(x, positions, w_in, b_glu, conv_w, conv_b, conv_ln_g, conv_ln_b, lam_q1, lam_k1, lam_q2, lam_k2,
           subln_g, w_o, ln1_g, ln1_b, w_gate, w_up, w_down, ln2_g, ln2_b):
    batch, seq, d = x.shape
    m = batch * seq
    cos_t, sin_t = _rope_tables(positions, m)
    x2 = x.reshape(m, d)
    for l in range(DEPTH):
        lambda_init = 0.8 - 0.6 * math.exp(-0.3 * l)
        x_bf = x2.astype(BF16)
        qkv, w_o_bf = _in_proj_qkv(x_bf, w_in[l], cos_t, sin_t, w_o[l])
        glu = _in_proj_glu(x_bf, w_in[l], b_glu[l])
        conv = _conv_ln_swish(glu.reshape(batch, seq, CONV_WIDTH), conv_w[l], conv_b[l], conv_ln_g[l], conv_ln_b[l])
        attn = _diff_attention(qkv, lam_q1[l], lam_k1[l], lam_q2[l], lam_k2[l], subln_g[l], batch, seq, lambda_init)
        h, h_bf = _out_proj_ln(attn, conv.reshape(m, CONV_WIDTH), w_o_bf, x2, ln1_g[l], ln1_b[l])
        hid, w_down_bf = _ffn_up(h_bf, w_gate[l], w_up[l], w_down[l])
        x2 = _ffn_down_ln(hid, w_down_bf, h, ln2_g[l], ln2_b[l])
    return x2.reshape(batch, seq, d)
```

```python
import functools
import math

import jax
import jax.numpy as jnp
from jax import lax
from jax.experimental import pallas as pl
from jax.experimental.pallas import tpu as pltpu

D_MODEL = 4096
DEPTH = 1
ATTN_WIDTH = D_MODEL // 2
CONV_WIDTH = D_MODEL - ATTN_WIDTH
HEAD_DIM = 128
V_HEAD_DIM = 2 * HEAD_DIM
N_DIFF_HEADS = ATTN_WIDTH // V_HEAD_DIM
CONV_TAPS = 31
ROPE_THETA = 500000.0
ROPE_DIM = HEAD_DIM // 4
ROPE_HALF = ROPE_DIM // 2
D_FF = -(-8 * D_MODEL // (3 * 256)) * 256
Q_COLS = N_DIFF_HEADS * 2 * HEAD_DIM
K_COLS = Q_COLS
V_COLS = N_DIFF_HEADS * V_HEAD_DIM
QKV_COLS = Q_COLS + K_COLS + V_COLS
GLU_COLS = 2 * CONV_WIDTH
DEEPNORM_ALPHA = (2 * DEPTH) ** 0.25
LN_EPS = 1e-5

V7X_LANES = 128
V7X_SUBLANES = 8
V7X_VMEM_BYTES = 64 * 1024 * 1024
V7X_VMEM_REQUEST_CAP = 60000 * 1024

F32 = jnp.float32
BF16 = jnp.bfloat16
NEG_BIG = -1e30


def _nbytes(shape, dtype):
    return math.prod(shape) * jnp.dtype(dtype).itemsize


def _vmem_limit(pipelined, resident=()):
    need = 2 * sum(_nbytes(s, d) for s, d in pipelined) + sum(_nbytes(s, d) for s, d in resident)
    return min(max(need + need // 4, 16 * 1024 * 1024), V7X_VMEM_REQUEST_CAP)


def _dot(a, b):
    return jnp.dot(a, b, preferred_element_type=F32)


def _layer_norm_rows(v, g, b):
    mu = jnp.mean(v, axis=-1, keepdims=True)
    vc = v - mu
    var = jnp.mean(vc * vc, axis=-1, keepdims=True)
    return vc * lax.rsqrt(var + LN_EPS) * g + b


LN_CHUNK_ROWS = 64
LN_COL_GROUP = 256


def _layer_norm_block(src_ref, g_ref, b_ref, dst_refs):
    width = src_ref.shape[1]
    groups = [slice(c0, c0 + LN_COL_GROUP) for c0 in range(0, width, LN_COL_GROUP)]

    def row_sums(rows, term):
        acc = term(src_ref[rows, groups[0]])
        for cols in groups[1:]:
            acc = acc + term(src_ref[rows, cols])
        return jnp.sum(acc, axis=-1, keepdims=True)

    def chunk(c, carry):
        rows = pl.ds(pl.multiple_of(c * LN_CHUNK_ROWS, LN_CHUNK_ROWS), LN_CHUNK_ROWS)
        mu = row_sums(rows, lambda v: v) * (1.0 / width)
        var = row_sums(rows, lambda v: (v - mu) * (v - mu)) * (1.0 / width)
        rstd = lax.rsqrt(var + LN_EPS)
        for cols in groups:
            y = (src_ref[rows, cols] - mu) * rstd * g_ref[:, cols] + b_ref[:, cols]
            for dst in dst_refs:
                dst[rows, cols] = y.astype(dst.dtype)
        return carry

    lax.fori_loop(0, src_ref.shape[0] // LN_CHUNK_ROWS, chunk, 0)


def _rope_table_kernel(pos_ref, inv_ref, cos_ref, sin_ref):
    ang = pos_ref[...].astype(F32) * inv_ref[...]
    lane = lax.broadcasted_iota(jnp.int32, ang.shape, 1)
    s = jnp.sin(ang)
    cos_ref[...] = jnp.cos(ang)
    sin_ref[...] = jnp.where(lane < ROPE_HALF, -s, s)


def _rope_tables(positions, m):
    tm = 1024
    inv_freq = ROPE_THETA ** (-jnp.arange(0, ROPE_DIM, 2, dtype=F32) / ROPE_DIM)
    inv_lanes = jnp.concatenate([inv_freq, inv_freq, jnp.zeros((V7X_LANES - ROPE_DIM,), F32)]).reshape(1, V7X_LANES)
    out = jax.ShapeDtypeStruct((m, V7X_LANES), F32)
    return pl.pallas_call(
        _rope_table_kernel,
        grid=(m // tm,),
        in_specs=[pl.BlockSpec((tm, 1), lambda i: (i, 0)),
                  pl.BlockSpec((1, V7X_LANES), lambda i: (0, 0))],
        out_specs=[pl.BlockSpec((tm, V7X_LANES), lambda i: (i, 0))] * 2,
        out_shape=[out, out],
        name="rope_tables",
    )(positions.reshape(m, 1), inv_lanes)


def _ahead_row(jj, i):
    return jnp.where(jj == 0, 0, i)


def _ahead_col(jj):
    return jnp.maximum(jj - 1, 0)


def _ahead_chunk_spec(chunk, tn, n_tiles, n_rows, col0=0):
    return pl.BlockSpec((chunk, tn), lambda jj, i: (jnp.where(jj == n_tiles, n_rows - 1, i),
                                                    col0 + jnp.minimum(jj, n_tiles - 1)))


def _ahead_convert(jj, i, chunk_refs, slot_refs):
    chunk = chunk_refs[0].shape[0]
    rows = pl.ds(pl.multiple_of(i * chunk, chunk), chunk)
    for src, dst in zip(chunk_refs, slot_refs):
        dst[jj % 2, rows, :] = src[...].astype(BF16)


def _side_spec(rows, width, n_rows, n_blocks):
    return pl.BlockSpec((rows, width), lambda jj, i: (jnp.minimum(jj * n_rows + i, n_blocks - 1), 0))


SIDE_ROWS = 64


def _qkv_kernel(x_ref, wchunk_ref, cos_ref, sin_ref, side_ref, o_ref, side_o_ref, wbf_ref, *, tn, n_rot_tiles):
    jj = pl.program_id(0)
    i = pl.program_id(1)

    def convert():
        side_o_ref[...] = side_ref[...].astype(side_o_ref.dtype)
        _ahead_convert(jj, i, (wchunk_ref,), (wbf_ref,))

    pl.when(jj == 0)(convert)

    @pl.when(jj > 0)
    def _():
        convert()
        acc = _dot(x_ref[...], wbf_ref[(jj - 1) % 2])

        @pl.when(jj - 1 < n_rot_tiles)
        def _():
            c = cos_ref[...]
            s = sin_ref[...]
            lane = lax.broadcasted_iota(jnp.int32, c.shape, 1)
            for g in range(tn // V7X_LANES):
                t = acc[:, g * V7X_LANES:(g + 1) * V7X_LANES]
                partner = jnp.where(lane < ROPE_HALF,
                                    pltpu.roll(t, V7X_LANES - ROPE_HALF, 1),
                                    pltpu.roll(t, ROPE_HALF, 1))
                o_ref[:, g * V7X_LANES:(g + 1) * V7X_LANES] = (t * c + partner * s).astype(o_ref.dtype)

        @pl.when(jj - 1 >= n_rot_tiles)
        def _():
            o_ref[...] = acc.astype(o_ref.dtype)


def _in_proj_qkv(x_bf, w_in, cos_t, sin_t, w_side):
    m, d = x_bf.shape
    tm, tn = 1024, 1024
    n_rows, n_tiles = m // tm, QKV_COLS // tn
    chunk = d // n_rows
    side_rows = 2 * SIDE_ROWS
    side_blocks = w_side.shape[0] // side_rows
    assert side_blocks <= (n_tiles + 1) * n_rows
    kern = functools.partial(_qkv_kernel, tn=tn, n_rot_tiles=(Q_COLS + K_COLS) // tn)
    row_map = lambda jj, i: (_ahead_row(jj, i), 0)
    side = _side_spec(side_rows, w_side.shape[1], n_rows, side_blocks)
    limit = _vmem_limit([((tm, d), BF16), ((chunk, tn), F32), ((tm, V7X_LANES), F32), ((tm, V7X_LANES), F32),
                         ((side_rows, w_side.shape[1]), F32), ((tm, tn), BF16), ((side_rows, w_side.shape[1]), BF16)],
                        [((2, d, tn), BF16), ((tm, tn), F32)])
    return pl.pallas_call(
        kern,
        grid=(n_tiles + 1, n_rows),
        in_specs=[pl.BlockSpec((tm, d), row_map),
                  _ahead_chunk_spec(chunk, tn, n_tiles, n_rows),
                  pl.BlockSpec((tm, V7X_LANES), row_map),
                  pl.BlockSpec((tm, V7X_LANES), row_map),
                  side],
        out_specs=[pl.BlockSpec((tm, tn), lambda jj, i: (_ahead_row(jj, i), _ahead_col(jj))), side],
        out_shape=[jax.ShapeDtypeStruct((m, QKV_COLS), BF16), jax.ShapeDtypeStruct(w_side.shape, BF16)],
        scratch_shapes=[pltpu.VMEM((2, d, tn), BF16)],
        compiler_params=pltpu.CompilerParams(dimension_semantics=("arbitrary", "arbitrary"), vmem_limit_bytes=limit),
        name="in_proj_qkv",
    )(x_bf, w_in, cos_t, sin_t, w_side)


def _glu_kernel(x_ref, wa_chunk_ref, wg_chunk_ref, ba_ref, bg_ref, o_ref, wa_bf_ref, wg_bf_ref):
    jj = pl.program_id(0)
    i = pl.program_id(1)
    convert = functools.partial(_ahead_convert, jj, i, (wa_chunk_ref, wg_chunk_ref), (wa_bf_ref, wg_bf_ref))
    pl.when(jj == 0)(convert)

    @pl.when(jj > 0)
    def _():
        convert()
        x = x_ref[...]
        slot = (jj - 1) % 2
        a = _dot(x, wa_bf_ref[slot]) + ba_ref[...]
        gate = _dot(x, wg_bf_ref[slot]) + bg_ref[...]
        o_ref[...] = (a * jax.nn.sigmoid(gate)).astype(o_ref.dtype)


def _in_proj_glu(x_bf, w_in, b_glu):
    m, d = x_bf.shape
    tm, tn = 1024, 512
    n_rows, n_tiles = m // tm, CONV_WIDTH // tn
    chunk = d // n_rows
    a0 = QKV_COLS // tn
    g0 = (QKV_COLS + CONV_WIDTH) // tn
    limit = _vmem_limit([((tm, d), BF16), ((chunk, tn), F32), ((chunk, tn), F32), ((tm, tn), F32)],
                        [((2, d, tn), BF16)] * 2 + [((tm, tn), F32)] * 2)
    return pl.pallas_call(
        _glu_kernel,
        grid=(n_tiles + 1, n_rows),
        in_specs=[pl.BlockSpec((tm, d), lambda jj, i: (_ahead_row(jj, i), 0)),
                  _ahead_chunk_spec(chunk, tn, n_tiles, n_rows, a0),
                  _ahead_chunk_spec(chunk, tn, n_tiles, n_rows, g0),
                  pl.BlockSpec((1, tn), lambda jj, i: (0, _ahead_col(jj))),
                  pl.BlockSpec((1, tn), lambda jj, i: (0, n_tiles + _ahead_col(jj)))],
        out_specs=pl.BlockSpec((tm, tn), lambda jj, i: (_ahead_row(jj, i), _ahead_col(jj))),
        out_shape=jax.ShapeDtypeStruct((m, CONV_WIDTH), F32),
        scratch_shapes=[pltpu.VMEM((2, d, tn), BF16)] * 2,
        compiler_params=pltpu.CompilerParams(dimension_semantics=("arbitrary", "arbitrary"), vmem_limit_bytes=limit),
        name="in_proj_glu",
    )(x_bf, w_in, w_in, b_glu.reshape(1, GLU_COLS), b_glu.reshape(1, GLU_COLS))


CONV_HALO = 32
CONV_ROWS = 128
CONV_BACK = V7X_SUBLANES * ((CONV_TAPS - 1) // V7X_SUBLANES)


def _conv_ln_kernel(cur_ref, prev_ref, w_ref, cb_ref, g_ref, b_ref, o_ref, ext_ref, y_ref, *, ts):
    i = pl.program_id(1)
    width = cur_ref.shape[-1]
    ext_ref[0:CONV_HALO, :] = jnp.where(i > 0, prev_ref[0], 0.0)
    ext_ref[CONV_HALO:, :] = cur_ref[0]

    def channel_group(cg, carry):
        c0 = pl.multiple_of(cg * V7X_LANES, V7X_LANES)
        lanes = pl.ds(c0, V7X_LANES)
        for r0 in range(0, ts, CONV_ROWS):
            acc = jnp.zeros((CONV_ROWS, V7X_LANES), F32)
            win = ext_ref[pl.ds(r0, CONV_ROWS + CONV_HALO), lanes]
            for r in range(V7X_SUBLANES):
                shifted = pltpu.roll(win, r, 0) if r else win
                for q in range(CONV_BACK // V7X_SUBLANES + 1):
                    u = V7X_SUBLANES * q + r
                    if u < CONV_TAPS:
                        lo = CONV_HALO - V7X_SUBLANES * q
                        acc = acc + shifted[lo:lo + CONV_ROWS, :] * w_ref[pl.ds(CONV_TAPS - 1 - u, 1), lanes]
            y_ref[pl.ds(r0, CONV_ROWS), lanes] = acc + cb_ref[:, lanes]
        return carry

    lax.fori_loop(0, width // V7X_LANES, channel_group, 0)
    z = _layer_norm_rows(y_ref[...], g_ref[...], b_ref[...])
    o_ref[0] = (z * jax.nn.sigmoid(z)).astype(o_ref.dtype)


def _conv_ln_swish(c, conv_w, conv_b, ln_g, ln_b):
    b, s, width = c.shape
    ts = 256
    per = ts // CONV_HALO
    kern = functools.partial(_conv_ln_kernel, ts=ts)
    row = lambda a: a.reshape(1, width)
    limit = _vmem_limit([((ts, width), F32), ((CONV_HALO, width), F32), ((CONV_TAPS, width), F32), ((ts, width), BF16)],
                        [((ts + CONV_HALO, width), F32), ((ts, width), F32), ((ts, width), F32)])
    return pl.pallas_call(
        kern,
        grid=(b, s // ts),
        in_specs=[pl.BlockSpec((1, ts, width), lambda bi, i: (bi, i, 0)),
                  pl.BlockSpec((1, CONV_HALO, width), lambda bi, i: (bi, jnp.maximum(i * per - 1, 0), 0)),
                  pl.BlockSpec((CONV_TAPS, width), lambda bi, i: (0, 0)),
                  pl.BlockSpec((1, width), lambda bi, i: (0, 0)),
                  pl.BlockSpec((1, width), lambda bi, i: (0, 0)),
                  pl.BlockSpec((1, width), lambda bi, i: (0, 0))],
        out_specs=pl.BlockSpec((1, ts, width), lambda bi, i: (bi, i, 0)),
        out_shape=jax.ShapeDtypeStruct((b, s, width), BF16),
        scratch_shapes=[pltpu.VMEM((ts + CONV_HALO, width), F32), pltpu.VMEM((ts, width), F32)],
        compiler_params=pltpu.CompilerParams(dimension_semantics=("parallel", "arbitrary"), vmem_limit_bytes=limit),
        name="conv_ln_swish",
    )(c, c, conv_w, row(conv_b), row(ln_g), row(ln_b))


ATTN_BLOCK = 512
SCORE_SCALE_LOG2 = HEAD_DIM ** -0.5 * math.log2(math.e)


def _dot_nt(a, b):
    return lax.dot_general(a, b, (((1,), (1,)), ((), ())), preferred_element_type=F32)


ATTN_HEADS_PER_STEP = 2


def _attn_kernel(q_ref, k_ref, v_ref, lq1_ref, lk1_ref, lq2_ref, lk2_ref, g_ref, o_ref, *, seq, lambda_init):
    blk = ATTN_BLOCK
    lam = (jnp.exp(jnp.sum(lq1_ref[...] * lk1_ref[...], axis=-1, keepdims=True))
           - jnp.exp(jnp.sum(lq2_ref[...] * lk2_ref[...], axis=-1, keepdims=True)) + lambda_init)
    row = lax.broadcasted_iota(jnp.int32, (blk, blk), 0)
    col = lax.broadcasted_iota(jnp.int32, (blk, blk), 1)
    causal = col <= row

    def softmax_times_v(qk_cols, v_cols, r0):
        q = q_ref[r0:r0 + blk, qk_cols]
        s_diag = jnp.where(causal, _dot_nt(q, k_ref[r0:r0 + blk, qk_cols]), NEG_BIG)
        m = jnp.max(s_diag, axis=-1, keepdims=True)
        if r0:
            s_past = _dot_nt(q, k_ref[0:r0, qk_cols])
            m = jnp.maximum(m, jnp.max(s_past, axis=-1, keepdims=True))
        p = jnp.exp2((s_diag - m) * SCORE_SCALE_LOG2)
        l = jnp.sum(p, axis=-1, keepdims=True)
        acc = _dot(p.astype(BF16), v_ref[r0:r0 + blk, v_cols])
        if r0:
            p = jnp.exp2((s_past - m) * SCORE_SCALE_LOG2)
            l = l + jnp.sum(p, axis=-1, keepdims=True)
            acc = acc + _dot(p.astype(BF16), v_ref[0:r0, v_cols])
        return acc / l

    for hh in range(ATTN_HEADS_PER_STEP):
        c1 = slice(2 * HEAD_DIM * hh, 2 * HEAD_DIM * hh + HEAD_DIM)
        c2 = slice(2 * HEAD_DIM * hh + HEAD_DIM, 2 * HEAD_DIM * (hh + 1))
        vc = slice(V_HEAD_DIM * hh, V_HEAD_DIM * (hh + 1))
        for r0 in range(0, seq, blk):
            o = softmax_times_v(c1, vc, r0) - lam * softmax_times_v(c2, vc, r0)
            y = o * lax.rsqrt(jnp.mean(o * o, axis=-1, keepdims=True) + LN_EPS)
            o_ref[r0:r0 + blk, vc] = ((y * g_ref[...]) * (1.0 - lambda_init)).astype(o_ref.dtype)


def _diff_attention(qkv, lam_q1, lam_k1, lam_q2, lam_k2, subln_g, batch, seq, lambda_init):
    m = qkv.shape[0]
    hps = ATTN_HEADS_PER_STEP
    kern = functools.partial(_attn_kernel, seq=seq, lambda_init=lambda_init)
    qk_w = hps * 2 * HEAD_DIM
    v_w = hps * V_HEAD_DIM
    kcol = Q_COLS // qk_w
    vcol = (Q_COLS + K_COLS) // v_w
    vec = lambda a: a.reshape(1, -1)
    vec_spec = lambda n: pl.BlockSpec((1, n), lambda b, h: (0, 0))
    limit = _vmem_limit([((seq, qk_w), BF16)] * 2 + [((seq, v_w), BF16)] * 2,
                        ([((ATTN_BLOCK, seq), F32)] * 3 + [((ATTN_BLOCK, seq), BF16)] * 2) * hps)
    return pl.pallas_call(
        kern,
        grid=(batch, N_DIFF_HEADS // hps),
        in_specs=[pl.BlockSpec((seq, qk_w), lambda b, h: (b, h)),
                  pl.BlockSpec((seq, qk_w), lambda b, h: (b, kcol + h)),
                  pl.BlockSpec((seq, v_w), lambda b, h: (b, vcol + h)),
                  vec_spec(HEAD_DIM), vec_spec(HEAD_DIM), vec_spec(HEAD_DIM), vec_spec(HEAD_DIM),
                  vec_spec(V_HEAD_DIM)],
        out_specs=pl.BlockSpec((seq, v_w), lambda b, h: (b, h)),
        out_shape=jax.ShapeDtypeStruct((m, ATTN_WIDTH), BF16),
        compiler_params=pltpu.CompilerParams(dimension_semantics=("parallel", "parallel"), vmem_limit_bytes=limit),
        name="diff_attention",
    )(qkv, qkv, qkv, vec(lam_q1), vec(lam_k1), vec(lam_q2), vec(lam_k2), vec(subln_g))


def _oproj_ln_kernel(a_ref, c_ref, wa_ref, wc_ref, x_ref, g_ref, b_ref, h_ref, hb_ref, *, tn, nj):
    j = pl.program_id(1)
    y = _dot(a_ref[...], wa_ref[...]) + _dot(c_ref[...], wc_ref[...]) + DEEPNORM_ALPHA * x_ref[...]
    h_ref[:, pl.ds(pl.multiple_of(j * tn, tn), tn)] = y

    @pl.when(j == nj - 1)
    def _():
        _layer_norm_block(h_ref, g_ref, b_ref, (h_ref, hb_ref))


def _out_proj_ln(attn, conv, w_o_bf, x2, ln_g, ln_b):
    m, d = x2.shape
    tm, tn = 512, 1024
    nj = d // tn
    kern = functools.partial(_oproj_ln_kernel, tn=tn, nj=nj)
    limit = _vmem_limit([((tm, ATTN_WIDTH), BF16), ((tm, CONV_WIDTH), BF16), ((ATTN_WIDTH, tn), BF16),
                         ((CONV_WIDTH, tn), BF16), ((tm, tn), F32), ((tm, d), F32), ((tm, d), BF16)],
                        [((tm, tn), F32)] * 2)
    return pl.pallas_call(
        kern,
        grid=(m // tm, nj),
        in_specs=[pl.BlockSpec((tm, ATTN_WIDTH), lambda i, j: (i, 0)),
                  pl.BlockSpec((tm, CONV_WIDTH), lambda i, j: (i, 0)),
                  pl.BlockSpec((ATTN_WIDTH, tn), lambda i, j: (0, j)),
                  pl.BlockSpec((CONV_WIDTH, tn), lambda i, j: (1, j)),
                  pl.BlockSpec((tm, tn), lambda i, j: (i, j)),
                  pl.BlockSpec((1, d), lambda i, j: (0, 0)),
                  pl.BlockSpec((1, d), lambda i, j: (0, 0))],
        out_specs=[pl.BlockSpec((tm, d), lambda i, j: (i, 0)),
                   pl.BlockSpec((tm, d), lambda i, j: (i, 0))],
        out_shape=[jax.ShapeDtypeStruct((m, d), F32), jax.ShapeDtypeStruct((m, d), BF16)],
        compiler_params=pltpu.CompilerParams(dimension_semantics=("parallel", "arbitrary"), vmem_limit_bytes=limit),
        name="out_proj_ln",
    )(attn, conv, w_o_bf, w_o_bf, x2, ln_g.reshape(1, d), ln_b.reshape(1, d))


FF_TILE = 512
D_FF_PAD = -(-D_FF // FF_TILE) * FF_TILE


def _ffn_up_kernel(h_ref, wg_chunk_ref, wu_chunk_ref, side_ref, o_ref, side_o_ref, wg_bf_ref, wu_bf_ref, *,
                   tn, n_tiles, last_cols, n_rows, side_valid_blocks):
    jj = pl.program_id(0)
    i = pl.program_id(1)

    def convert():
        side_o_ref[...] = jnp.where(jj * n_rows + i < side_valid_blocks, side_ref[...], 0.0).astype(side_o_ref.dtype)
        _ahead_convert(jj, i, (wg_chunk_ref, wu_chunk_ref), (wg_bf_ref, wu_bf_ref))

    pl.when(jj == 0)(convert)

    def tile(ncols):
        convert()
        h = h_ref[...]
        slot = (jj - 1) % 2
        g = _dot(h, wg_bf_ref[slot, :, 0:ncols])
        u = _dot(h, wu_bf_ref[slot, :, 0:ncols])
        o_ref[:, 0:ncols] = ((g * jax.nn.sigmoid(g)) * u).astype(o_ref.dtype)
        if ncols < tn:
            o_ref[:, ncols:] = jnp.zeros((o_ref.shape[0], tn - ncols), o_ref.dtype)

    pl.when(jnp.logical_and(jj > 0, jj < n_tiles))(functools.partial(tile, tn))
    pl.when(jj == n_tiles)(functools.partial(tile, last_cols))


def _ffn_up(h_bf, w_gate, w_up, w_side):
    m, d = h_bf.shape
    tm, tn = 1024, FF_TILE
    n_rows, n_tiles = m // tm, D_FF_PAD // tn
    chunk = d // n_rows
    side_valid_blocks = w_side.shape[0] // SIDE_ROWS
    side_blocks = D_FF_PAD // SIDE_ROWS
    assert side_blocks <= (n_tiles + 1) * n_rows and side_valid_blocks * SIDE_ROWS == w_side.shape[0]
    kern = functools.partial(_ffn_up_kernel, tn=tn, n_tiles=n_tiles, last_cols=D_FF - (n_tiles - 1) * tn,
                             n_rows=n_rows, side_valid_blocks=side_valid_blocks)
    width = w_side.shape[1]
    limit = _vmem_limit([((tm, d), BF16), ((chunk, tn), F32), ((chunk, tn), F32), ((SIDE_ROWS, width), F32),
                         ((tm, tn), BF16), ((SIDE_ROWS, width), BF16)],
                        [((2, d, tn), BF16)] * 2 + [((tm, tn), F32)] * 3)
    return pl.pallas_call(
        kern,
        grid=(n_tiles + 1, n_rows),
        in_specs=[pl.BlockSpec((tm, d), lambda jj, i: (_ahead_row(jj, i), 0)),
                  _ahead_chunk_spec(chunk, tn, n_tiles, n_rows),
                  _ahead_chunk_spec(chunk, tn, n_tiles, n_rows),
                  _side_spec(SIDE_ROWS, width, n_rows, side_valid_blocks)],
        out_specs=[pl.BlockSpec((tm, tn), lambda jj, i: (_ahead_row(jj, i), _ahead_col(jj))),
                   _side_spec(SIDE_ROWS, width, n_rows, side_blocks)],
        out_shape=[jax.ShapeDtypeStruct((m, D_FF_PAD), BF16), jax.ShapeDtypeStruct((D_FF_PAD, width), BF16)],
        scratch_shapes=[pltpu.VMEM((2, d, tn), BF16)] * 2,
        compiler_params=pltpu.CompilerParams(dimension_semantics=("arbitrary", "arbitrary"), vmem_limit_bytes=limit),
        name="ffn_up",
    )(h_bf, w_gate, w_up, w_side)


DOWN_COL_CHUNK = 1024


DOWN_RES_COLS = 512


def _ffn_down_ln_kernel(hid_ref, w_ref, h_ref, g_ref, b_ref, o_ref, *, nk, n_res, last_rows):
    k = pl.program_id(1)

    def accumulate(kk, first):
        hid = hid_ref[:, 0:kk]
        for n0 in range(0, o_ref.shape[1], DOWN_COL_CHUNK):
            part = _dot(hid, w_ref[0:kk, n0:n0 + DOWN_COL_CHUNK])
            if first:
                o_ref[:, n0:n0 + DOWN_COL_CHUNK] = part
            else:
                o_ref[:, n0:n0 + DOWN_COL_CHUNK] += part

    pl.when(k == 0)(functools.partial(accumulate, hid_ref.shape[1], True))
    pl.when(jnp.logical_and(k > 0, k < nk - 1))(functools.partial(accumulate, hid_ref.shape[1], False))
    pl.when(k == nk - 1)(functools.partial(accumulate, last_rows, False))

    @pl.when(k < n_res)
    def _():
        cols = pl.ds(pl.multiple_of(k * DOWN_RES_COLS, DOWN_RES_COLS), DOWN_RES_COLS)
        o_ref[:, cols] += DEEPNORM_ALPHA * h_ref[...]

    @pl.when(k == nk - 1)
    def _():
        _layer_norm_block(o_ref, g_ref, b_ref, (o_ref,))


def _ffn_down_ln(hid, w_down_bf, h, ln_g, ln_b):
    m, d = h.shape
    tm, tk = 1024, 512
    nk = D_FF_PAD // tk
    n_res = d // DOWN_RES_COLS
    assert n_res <= nk
    kern = functools.partial(_ffn_down_ln_kernel, nk=nk, n_res=n_res, last_rows=D_FF - (nk - 1) * tk)
    limit = _vmem_limit([((tm, tk), BF16), ((tk, d), BF16), ((tm, DOWN_RES_COLS), F32), ((tm, d), F32)],
                        [((tm, DOWN_COL_CHUNK), F32)])
    return pl.pallas_call(
        kern,
        grid=(m // tm, nk),
        in_specs=[pl.BlockSpec((tm, tk), lambda i, k: (i, k)),
                  pl.BlockSpec((tk, d), lambda i, k: (k, 0)),
                  pl.BlockSpec((tm, DOWN_RES_COLS), lambda i, k: (i, jnp.minimum(k, n_res - 1))),
                  pl.BlockSpec((1, d), lambda i, k: (0, 0)),
                  pl.BlockSpec((1, d), lambda i, k: (0, 0))],
        out_specs=pl.BlockSpec((tm, d), lambda i, k: (i, 0)),
        out_shape=jax.ShapeDtypeStruct((m, d), F32),
        compiler_params=pltpu.CompilerParams(dimension_semantics=("parallel", "arbitrary"), vmem_limit_bytes=limit),
        name="ffn_down_ln",
    )(hid, w_down_bf, h, ln_g.reshape(1, d), ln_b.reshape(1, d))


def kernel(x, positions, w_in, b_glu, conv_w, conv_b, conv_ln_g, conv_ln_b, lam_q1, lam_k1, lam_q2, lam_k2,
           subln_g, w_o, ln1_g, ln1_b, w_gate, w_up, w_down, ln2_g, ln2_b):
    batch, seq, d = x.shape
    m = batch * seq
    cos_t, sin_t = _rope_tables(positions, m)
    x2 = x.reshape(m, d)
    for l in range(DEPTH):
        lambda_init = 0.8 - 0.6 * math.exp(-0.3 * l)
        x_bf = x2.astype(BF16)
        qkv, w_o_bf = _in_proj_qkv(x_bf, w_in[l], cos_t, sin_t, w_o[l])
        glu = _in_proj_glu(x_bf, w_in[l], b_glu[l])
        conv = _conv_ln_swish(glu.reshape(batch, seq, CONV_WIDTH), conv_w[l], conv_b[l], conv_ln_g[l], conv_ln_b[l])
        attn = _diff_attention(qkv, lam_q1[l], lam_k1[l], lam_q2[l], lam_k2[l], subln_g[l], batch, seq, lambda_init)
        h, h_bf = _out_proj_ln(attn, conv.reshape(m, CONV_WIDTH), w_o_bf, x2, ln1_g[l], ln1_b[l])
        hid, w_down_bf = _ffn_up(h_bf, w_gate[l], w_up[l], w_down[l])
        x2 = _ffn_down_ln(hid, w_down_bf, h, ln2_g[l], ln2_b[l])
    return x2.reshape(batch, seq, d)
```

```python
import functools
import math

import jax
import jax.numpy as jnp
from jax import lax
from jax.experimental import pallas as pl
from jax.experimental.pallas import tpu as pltpu

D_MODEL = 4096
DEPTH = 1
ATTN_WIDTH = D_MODEL // 2
CONV_WIDTH = D_MODEL - ATTN_WIDTH
HEAD_DIM = 128
V_HEAD_DIM = 2 * HEAD_DIM
N_DIFF_HEADS = ATTN_WIDTH // V_HEAD_DIM
CONV_TAPS = 31
ROPE_THETA = 500000.0
ROPE_DIM = HEAD_DIM // 4
ROPE_HALF = ROPE_DIM // 2
D_FF = -(-8 * D_MODEL // (3 * 256)) * 256
Q_COLS = N_DIFF_HEADS * 2 * HEAD_DIM
K_COLS = Q_COLS
V_COLS = N_DIFF_HEADS * V_HEAD_DIM
QKV_COLS = Q_COLS + K_COLS + V_COLS
GLU_COLS = 2 * CONV_WIDTH
DEEPNORM_ALPHA = (2 * DEPTH) ** 0.25
LN_EPS = 1e-5

V7X_LANES = 128
V7X_SUBLANES = 8
V7X_VMEM_BYTES = 64 * 1024 * 1024
V7X_VMEM_REQUEST_CAP = 60000 * 1024

F32 = jnp.float32
BF16 = jnp.bfloat16
NEG_BIG = -1e30


def _nbytes(shape, dtype):
    return math.prod(shape) * jnp.dtype(dtype).itemsize


def _vmem_limit(pipelined, resident=()):
    need = 2 * sum(_nbytes(s, d) for s, d in pipelined) + sum(_nbytes(s, d) for s, d in resident)
    return min(max(need + need // 4, 16 * 1024 * 1024), V7X_VMEM_REQUEST_CAP)


def _dot(a, b):
    return jnp.dot(a, b, preferred_element_type=F32)


def _layer_norm_rows(v, g, b):
    mu = jnp.mean(v, axis=-1, keepdims=True)
    vc = v - mu
    var = jnp.mean(vc * vc, axis=-1, keepdims=True)
    return vc * lax.rsqrt(var + LN_EPS) * g + b


LN_CHUNK_ROWS = 128
LN_COL_GROUP = 256


def _layer_norm_block(src_ref, g_ref, b_ref, dst_refs):
    width = src_ref.shape[1]
    groups = [slice(c0, c0 + LN_COL_GROUP) for c0 in range(0, width, LN_COL_GROUP)]

    def row_sums(rows, term):
        acc = term(src_ref[rows, groups[0]])
        for cols in groups[1:]:
            acc = acc + term(src_ref[rows, cols])
        return jnp.sum(acc, axis=-1, keepdims=True)

    def chunk(c, carry):
        rows = pl.ds(pl.multiple_of(c * LN_CHUNK_ROWS, LN_CHUNK_ROWS), LN_CHUNK_ROWS)
        mu = row_sums(rows, lambda v: v) * (1.0 / width)
        var = row_sums(rows, lambda v: (v - mu) * (v - mu)) * (1.0 / width)
        rstd = lax.rsqrt(var + LN_EPS)
        for cols in groups:
            y = (src_ref[rows, cols] - mu) * rstd * g_ref[:, cols] + b_ref[:, cols]
            for dst in dst_refs:
                dst[rows, cols] = y.astype(dst.dtype)
        return carry

    lax.fori_loop(0, src_ref.shape[0] // LN_CHUNK_ROWS, chunk, 0)


def _rope_table_kernel(pos_ref, inv_ref, cos_ref, sin_ref):
    ang = pos_ref[...].astype(F32) * inv_ref[...]
    lane = lax.broadcasted_iota(jnp.int32, ang.shape, 1)
    s = jnp.sin(ang)
    cos_ref[...] = jnp.cos(ang)
    sin_ref[...] = jnp.where(lane < ROPE_HALF, -s, s)


def _rope_tables(positions, m):
    tm = 1024
    inv_freq = ROPE_THETA ** (-jnp.arange(0, ROPE_DIM, 2, dtype=F32) / ROPE_DIM)
    inv_lanes = jnp.concatenate([inv_freq, inv_freq, jnp.zeros((V7X_LANES - ROPE_DIM,), F32)]).reshape(1, V7X_LANES)
    out = jax.ShapeDtypeStruct((m, V7X_LANES), F32)
    return pl.pallas_call(
        _rope_table_kernel,
        grid=(m // tm,),
        in_specs=[pl.BlockSpec((tm, 1), lambda i: (i, 0)),
                  pl.BlockSpec((1, V7X_LANES), lambda i: (0, 0))],
        out_specs=[pl.BlockSpec((tm, V7X_LANES), lambda i: (i, 0))] * 2,
        out_shape=[out, out],
        name="rope_tables",
    )(positions.reshape(m, 1), inv_lanes)


def _ahead_row(jj, i):
    return jnp.where(jj == 0, 0, i)


def _ahead_col(jj):
    return jnp.maximum(jj - 1, 0)


def _ahead_chunk_spec(chunk, tn, n_tiles, n_rows, col0=0):
    return pl.BlockSpec((chunk, tn), lambda jj, i: (jnp.where(jj == n_tiles, n_rows - 1, i),
                                                    col0 + jnp.minimum(jj, n_tiles - 1)))


def _ahead_convert(jj, i, chunk_refs, slot_refs):
    chunk = chunk_refs[0].shape[0]
    rows = pl.ds(pl.multiple_of(i * chunk, chunk), chunk)
    for src, dst in zip(chunk_refs, slot_refs):
        dst[jj % 2, rows, :] = src[...].astype(BF16)


def _side_spec(rows, width, n_rows, n_blocks):
    return pl.BlockSpec((rows, width), lambda jj, i: (jnp.minimum(jj * n_rows + i, n_blocks - 1), 0))


SIDE_ROWS = 64


def _qkv_kernel(x_ref, wchunk_ref, cos_ref, sin_ref, side_ref, o_ref, side_o_ref, wbf_ref, *, tn, n_rot_tiles):
    jj = pl.program_id(0)
    i = pl.program_id(1)

    def convert():
        side_o_ref[...] = side_ref[...].astype(side_o_ref.dtype)
        _ahead_convert(jj, i, (wchunk_ref,), (wbf_ref,))

    pl.when(jj == 0)(convert)

    @pl.when(jj > 0)
    def _():
        convert()
        acc = _dot(x_ref[...], wbf_ref[(jj - 1) % 2])

        @pl.when(jj - 1 < n_rot_tiles)
        def _():
            c = cos_ref[...]
            s = sin_ref[...]
            lane = lax.broadcasted_iota(jnp.int32, c.shape, 1)
            for g in range(tn // V7X_LANES):
                t = acc[:, g * V7X_LANES:(g + 1) * V7X_LANES]
                partner = jnp.where(lane < ROPE_HALF,
                                    pltpu.roll(t, V7X_LANES - ROPE_HALF, 1),
                                    pltpu.roll(t, ROPE_HALF, 1))
                o_ref[:, g * V7X_LANES:(g + 1) * V7X_LANES] = (t * c + partner * s).astype(o_ref.dtype)

        @pl.when(jj - 1 >= n_rot_tiles)
        def _():
            o_ref[...] = acc.astype(o_ref.dtype)


def _in_proj_qkv(x_bf, w_in, cos_t, sin_t, w_side):
    m, d = x_bf.shape
    tm, tn = 1024, 1024
    n_rows, n_tiles = m // tm, QKV_COLS // tn
    chunk = d // n_rows
    side_rows = 2 * SIDE_ROWS
    side_blocks = w_side.shape[0] // side_rows
    assert side_blocks <= (n_tiles + 1) * n_rows
    kern = functools.partial(_qkv_kernel, tn=tn, n_rot_tiles=(Q_COLS + K_COLS) // tn)
    row_map = lambda jj, i: (_ahead_row(jj, i), 0)
    side = _side_spec(side_rows, w_side.shape[1], n_rows, side_blocks)
    limit = _vmem_limit([((tm, d), BF16), ((chunk, tn), F32), ((tm, V7X_LANES), F32), ((tm, V7X_LANES), F32),
                         ((side_rows, w_side.shape[1]), F32), ((tm, tn), BF16), ((side_rows, w_side.shape[1]), BF16)],
                        [((2, d, tn), BF16), ((tm, tn), F32)])
    return pl.pallas_call(
        kern,
        grid=(n_tiles + 1, n_rows),
        in_specs=[pl.BlockSpec((tm, d), row_map),
                  _ahead_chunk_spec(chunk, tn, n_tiles, n_rows),
                  pl.BlockSpec((tm, V7X_LANES), row_map),
                  pl.BlockSpec((tm, V7X_LANES), row_map),
                  side],
        out_specs=[pl.BlockSpec((tm, tn), lambda jj, i: (_ahead_row(jj, i), _ahead_col(jj))), side],
        out_shape=[jax.ShapeDtypeStruct((m, QKV_COLS), BF16), jax.ShapeDtypeStruct(w_side.shape, BF16)],
        scratch_shapes=[pltpu.VMEM((2, d, tn), BF16)],
        compiler_params=pltpu.CompilerParams(dimension_semantics=("arbitrary", "arbitrary"), vmem_limit_bytes=limit),
        name="in_proj_qkv",
    )(x_bf, w_in, cos_t, sin_t, w_side)


def _glu_kernel(x_ref, wa_chunk_ref, wg_chunk_ref, ba_ref, bg_ref, o_ref, wa_bf_ref, wg_bf_ref):
    jj = pl.program_id(0)
    i = pl.program_id(1)
    convert = functools.partial(_ahead_convert, jj, i, (wa_chunk_ref, wg_chunk_ref), (wa_bf_ref, wg_bf_ref))
    pl.when(jj == 0)(convert)

    @pl.when(jj > 0)
    def _():
        convert()
        x = x_ref[...]
        slot = (jj - 1) % 2
        a = _dot(x, wa_bf_ref[slot]) + ba_ref[...]
        gate = _dot(x, wg_bf_ref[slot]) + bg_ref[...]
        o_ref[...] = (a * jax.nn.sigmoid(gate)).astype(o_ref.dtype)


def _in_proj_glu(x_bf, w_in, b_glu):
    m, d = x_bf.shape
    tm, tn = 1024, 512
    n_rows, n_tiles = m // tm, CONV_WIDTH // tn
    chunk = d // n_rows
    a0 = QKV_COLS // tn
    g0 = (QKV_COLS + CONV_WIDTH) // tn
    limit = _vmem_limit([((tm, d), BF16), ((chunk, tn), F32), ((chunk, tn), F32), ((tm, tn), F32)],
                        [((2, d, tn), BF16)] * 2 + [((tm, tn), F32)] * 2)
    return pl.pallas_call(
        _glu_kernel,
        grid=(n_tiles + 1, n_rows),
        in_specs=[pl.BlockSpec((tm, d), lambda jj, i: (_ahead_row(jj, i), 0)),
                  _ahead_chunk_spec(chunk, tn, n_tiles, n_rows, a0),
                  _ahead_chunk_spec(chunk, tn, n_tiles, n_rows, g0),
                  pl.BlockSpec((1, tn), lambda jj, i: (0, _ahead_col(jj))),
                  pl.BlockSpec((1, tn), lambda jj, i: (0, n_tiles + _ahead_col(jj)))],
        out_specs=pl.BlockSpec((tm, tn), lambda jj, i: (_ahead_row(jj, i), _ahead_col(jj))),
        out_shape=jax.ShapeDtypeStruct((m, CONV_WIDTH), F32),
        scratch_shapes=[pltpu.VMEM((2, d, tn), BF16)] * 2,
        compiler_params=pltpu.CompilerParams(dimension_semantics=("arbitrary", "arbitrary"), vmem_limit_bytes=limit),
        name="in_proj_glu",
    )(x_bf, w_in, w_in, b_glu.reshape(1, GLU_COLS), b_glu.reshape(1, GLU_COLS))


CONV_HALO = 32
CONV_ROWS = 128
CONV_BACK = V7X_SUBLANES * ((CONV_TAPS - 1) // V7X_SUBLANES)


def _conv_ln_kernel(cur_ref, prev_ref, w_ref, cb_ref, g_ref, b_ref, o_ref, ext_ref, y_ref, *, ts):
    i = pl.program_id(1)
    width = cur_ref.shape[-1]
    ext_ref[0:CONV_HALO, :] = jnp.where(i > 0, prev_ref[0], 0.0)
    ext_ref[CONV_HALO:, :] = cur_ref[0]

    def channel_group(cg, carry):
        c0 = pl.multiple_of(cg * V7X_LANES, V7X_LANES)
        lanes = pl.ds(c0, V7X_LANES)
        for r0 in range(0, ts, CONV_ROWS):
            acc = jnp.zeros((CONV_ROWS, V7X_LANES), F32)
            win = ext_ref[pl.ds(r0, CONV_ROWS + CONV_HALO), lanes]
            for r in range(V7X_SUBLANES):
                shifted = pltpu.roll(win, r, 0) if r else win
                for q in range(CONV_BACK // V7X_SUBLANES + 1):
                    u = V7X_SUBLANES * q + r
                    if u < CONV_TAPS:
                        lo = CONV_HALO - V7X_SUBLANES * q
                        acc = acc + shifted[lo:lo + CONV_ROWS, :] * w_ref[pl.ds(CONV_TAPS - 1 - u, 1), lanes]
            y_ref[pl.ds(r0, CONV_ROWS), lanes] = acc + cb_ref[:, lanes]
        return carry

    lax.fori_loop(0, width // V7X_LANES, channel_group, 0)
    z = _layer_norm_rows(y_ref[...], g_ref[...], b_ref[...])
    o_ref[0] = (z * jax.nn.sigmoid(z)).astype(o_ref.dtype)


def _conv_ln_swish(c, conv_w, conv_b, ln_g, ln_b):
    b, s, width = c.shape
    ts = 256
    per = ts // CONV_HALO
    kern = functools.partial(_conv_ln_kernel, ts=ts)
    row = lambda a: a.reshape(1, width)
    limit = _vmem_limit([((ts, width), F32), ((CONV_HALO, width), F32), ((CONV_TAPS, width), F32), ((ts, width), BF16)],
                        [((ts + CONV_HALO, width), F32), ((ts, width), F32), ((ts, width), F32)])
    return pl.pallas_call(
        kern,
        grid=(b, s // ts),
        in_specs=[pl.BlockSpec((1, ts, width), lambda bi, i: (bi, i, 0)),
                  pl.BlockSpec((1, CONV_HALO, width), lambda bi, i: (bi, jnp.maximum(i * per - 1, 0), 0)),
                  pl.BlockSpec((CONV_TAPS, width), lambda bi, i: (0, 0)),
                  pl.BlockSpec((1, width), lambda bi, i: (0, 0)),
                  pl.BlockSpec((1, width), lambda bi, i: (0, 0)),
                  pl.BlockSpec((1, width), lambda bi, i: (0, 0))],
        out_specs=pl.BlockSpec((1, ts, width), lambda bi, i: (bi, i, 0)),
        out_shape=jax.ShapeDtypeStruct((b, s, width), BF16),
        scratch_shapes=[pltpu.VMEM((ts + CONV_HALO, width), F32), pltpu.VMEM((ts, width), F32)],
        compiler_params=pltpu.CompilerParams(dimension_semantics=("parallel", "arbitrary"), vmem_limit_bytes=limit),
        name="conv_ln_swish",
    )(c, c, conv_w, row(conv_b), row(ln_g), row(ln_b))


ATTN_BLOCK = 512
ATTN_HEADS_PER_STEP = 2
SCORE_SCALE_LOG2 = HEAD_DIM ** -0.5 * math.log2(math.e)


def _dot_nt(a, b):
    return lax.dot_general(a, b, (((1,), (1,)), ((), ())), preferred_element_type=F32)


def _attn_kernel(q_ref, k_ref, v_ref, lq1_ref, lk1_ref, lq2_ref, lk2_ref, g_ref, o_ref, *, seq, lambda_init):
    blk = ATTN_BLOCK
    lam = (jnp.exp(jnp.sum(lq1_ref[...] * lk1_ref[...], axis=-1, keepdims=True))
           - jnp.exp(jnp.sum(lq2_ref[...] * lk2_ref[...], axis=-1, keepdims=True)) + lambda_init)
    row = lax.broadcasted_iota(jnp.int32, (blk, blk), 0)
    col = lax.broadcasted_iota(jnp.int32, (blk, blk), 1)
    causal = col <= row

    def softmax_times_v(qk_cols, v_cols, r0):
        q = q_ref[r0:r0 + blk, qk_cols]
        s_diag = jnp.where(causal, _dot_nt(q, k_ref[r0:r0 + blk, qk_cols]), NEG_BIG)
        m = jnp.max(s_diag, axis=-1, keepdims=True)
        if r0:
            s_past = _dot_nt(q, k_ref[0:r0, qk_cols])
            m = jnp.maximum(m, jnp.max(s_past, axis=-1, keepdims=True))
        p = jnp.exp2((s_diag - m) * SCORE_SCALE_LOG2)
        l = jnp.sum(p, axis=-1, keepdims=True)
        acc = _dot(p.astype(BF16), v_ref[r0:r0 + blk, v_cols])
        if r0:
            p = jnp.exp2((s_past - m) * SCORE_SCALE_LOG2)
            l = l + jnp.sum(p, axis=-1, keepdims=True)
            acc = acc + _dot(p.astype(BF16), v_ref[0:r0, v_cols])
        return acc / l

    for hh in range(ATTN_HEADS_PER_STEP):
        c1 = slice(2 * HEAD_DIM * hh, 2 * HEAD_DIM * hh + HEAD_DIM)
        c2 = slice(2 * HEAD_DIM * hh + HEAD_DIM, 2 * HEAD_DIM * (hh + 1))
        vc = slice(V_HEAD_DIM * hh, V_HEAD_DIM * (hh + 1))
        for r0 in range(0, seq, blk):
            o = softmax_times_v(c1, vc, r0) - lam * softmax_times_v(c2, vc, r0)
            y = o * lax.rsqrt(jnp.mean(o * o, axis=-1, keepdims=True) + LN_EPS)
            o_ref[r0:r0 + blk, vc] = ((y * g_ref[...]) * (1.0 - lambda_init)).astype(o_ref.dtype)


def _diff_attention(qkv, lam_q1, lam_k1, lam_q2, lam_k2, subln_g, batch, seq, lambda_init):
    m = qkv.shape[0]
    hps = ATTN_HEADS_PER_STEP
    kern = functools.partial(_attn_kernel, seq=seq, lambda_init=lambda_init)
    qk_w = hps * 2 * HEAD_DIM
    v_w = hps * V_HEAD_DIM
    kcol = Q_COLS // qk_w
    vcol = (Q_COLS + K_COLS) // v_w
    vec = lambda a: a.reshape(1, -1)
    vec_spec = lambda n: pl.BlockSpec((1, n), lambda b, h: (0, 0))
    limit = _vmem_limit([((seq, qk_w), BF16)] * 2 + [((seq, v_w), BF16)] * 2,
                        ([((ATTN_BLOCK, seq), F32)] * 3 + [((ATTN_BLOCK, seq), BF16)] * 2) * hps)
    return pl.pallas_call(
        kern,
        grid=(batch, N_DIFF_HEADS // hps),
        in_specs=[pl.BlockSpec((seq, qk_w), lambda b, h: (b, h)),
                  pl.BlockSpec((seq, qk_w), lambda b, h: (b, kcol + h)),
                  pl.BlockSpec((seq, v_w), lambda b, h: (b, vcol + h)),
                  vec_spec(HEAD_DIM), vec_spec(HEAD_DIM), vec_spec(HEAD_DIM), vec_spec(HEAD_DIM),
                  vec_spec(V_HEAD_DIM)],
        out_specs=pl.BlockSpec((seq, v_w), lambda b, h: (b, h)),
        out_shape=jax.ShapeDtypeStruct((m, ATTN_WIDTH), BF16),
        compiler_params=pltpu.CompilerParams(dimension_semantics=("parallel", "parallel"), vmem_limit_bytes=limit),
        name="diff_attention",
    )(qkv, qkv, qkv, vec(lam_q1), vec(lam_k1), vec(lam_q2), vec(lam_k2), vec(subln_g))


def _oproj_ln_kernel(a_ref, c_ref, wa_ref, wc_ref, x_ref, g_ref, b_ref, h_ref, hb_ref, *, tn, nj):
    j = pl.program_id(1)
    y = _dot(a_ref[...], wa_ref[...]) + _dot(c_ref[...], wc_ref[...]) + DEEPNORM_ALPHA * x_ref[...]
    h_ref[:, pl.ds(pl.multiple_of(j * tn, tn), tn)] = y

    @pl.when(j == nj - 1)
    def _():
        _layer_norm_block(h_ref, g_ref, b_ref, (h_ref, hb_ref))


def _out_proj_ln(attn, conv, w_o_bf, x2, ln_g, ln_b):
    m, d = x2.shape
    tm, tn = 512, 1024
    nj = d // tn
    kern = functools.partial(_oproj_ln_kernel, tn=tn, nj=nj)
    limit = _vmem_limit([((tm, ATTN_WIDTH), BF16), ((tm, CONV_WIDTH), BF16), ((ATTN_WIDTH, tn), BF16),
                         ((CONV_WIDTH, tn), BF16), ((tm, tn), F32), ((tm, d), F32), ((tm, d), BF16)],
                        [((tm, tn), F32)] * 2)
    return pl.pallas_call(
        kern,
        grid=(m // tm, nj),
        in_specs=[pl.BlockSpec((tm, ATTN_WIDTH), lambda i, j: (i, 0)),
                  pl.BlockSpec((tm, CONV_WIDTH), lambda i, j: (i, 0)),
                  pl.BlockSpec((ATTN_WIDTH, tn), lambda i, j: (0, j)),
                  pl.BlockSpec((CONV_WIDTH, tn), lambda i, j: (1, j)),
                  pl.BlockSpec((tm, tn), lambda i, j: (i, j)),
                  pl.BlockSpec((1, d), lambda i, j: (0, 0)),
                  pl.BlockSpec((1, d), lambda i, j: (0, 0))],
        out_specs=[pl.BlockSpec((tm, d), lambda i, j: (i, 0)),
                   pl.BlockSpec((tm, d), lambda i, j: (i, 0))],
        out_shape=[jax.ShapeDtypeStruct((m, d), F32), jax.ShapeDtypeStruct((m, d), BF16)],
        compiler_params=pltpu.CompilerParams(dimension_semantics=("parallel", "arbitrary"), vmem_limit_bytes=limit),
        name="out_proj_ln",
    )(attn, conv, w_o_bf, w_o_bf, x2, ln_g.reshape(1, d), ln_b.reshape(1, d))


FF_TILE = 512
D_FF_PAD = -(-D_FF // FF_TILE) * FF_TILE


def _ffn_up_kernel(h_ref, wg_chunk_ref, wu_chunk_ref, side_ref, o_ref, side_o_ref, wg_bf_ref, wu_bf_ref, *,
                   tn, n_tiles, last_cols, n_rows, side_valid_blocks):
    jj = pl.program_id(0)
    i = pl.program_id(1)

    def convert():
        side_o_ref[...] = jnp.where(jj * n_rows + i < side_valid_blocks, side_ref[...], 0.0).astype(side_o_ref.dtype)
        _ahead_convert(jj, i, (wg_chunk_ref, wu_chunk_ref), (wg_bf_ref, wu_bf_ref))

    pl.when(jj == 0)(convert)

    def tile(ncols):
        convert()
        h = h_ref[...]
        slot = (jj - 1) % 2
        g = _dot(h, wg_bf_ref[slot, :, 0:ncols])
        u = _dot(h, wu_bf_ref[slot, :, 0:ncols])
        o_ref[:, 0:ncols] = ((g * jax.nn.sigmoid(g)) * u).astype(o_ref.dtype)
        if ncols < tn:
            o_ref[:, ncols:] = jnp.zeros((o_ref.shape[0], tn - ncols), o_ref.dtype)

    pl.when(jnp.logical_and(jj > 0, jj < n_tiles))(functools.partial(tile, tn))
    pl.when(jj == n_tiles)(functools.partial(tile, last_cols))


def _ffn_up(h_bf, w_gate, w_up, w_side):
    m, d = h_bf.shape
    tm, tn = 1024, FF_TILE
    n_rows, n_tiles = m // tm, D_FF_PAD // tn
    chunk = d // n_rows
    side_valid_blocks = w_side.shape[0] // SIDE_ROWS
    side_blocks = D_FF_PAD // SIDE_ROWS
    assert side_blocks <= (n_tiles + 1) * n_rows and side_valid_blocks * SIDE_ROWS == w_side.shape[0]
    kern = functools.partial(_ffn_up_kernel, tn=tn, n_tiles=n_tiles, last_cols=D_FF - (n_tiles - 1) * tn,
                             n_rows=n_rows, side_valid_blocks=side_valid_blocks)
    width = w_side.shape[1]
    limit = _vmem_limit([((tm, d), BF16), ((chunk, tn), F32), ((chunk, tn), F32), ((SIDE_ROWS, width), F32),
                         ((tm, tn), BF16), ((SIDE_ROWS, width), BF16)],
                        [((2, d, tn), BF16)] * 2 + [((tm, tn), F32)] * 3)
    return pl.pallas_call(
        kern,
        grid=(n_tiles + 1, n_rows),
        in_specs=[pl.BlockSpec((tm, d), lambda jj, i: (_ahead_row(jj, i), 0)),
                  _ahead_chunk_spec(chunk, tn, n_tiles, n_rows),
                  _ahead_chunk_spec(chunk, tn, n_tiles, n_rows),
                  _side_spec(SIDE_ROWS, width, n_rows, side_valid_blocks)],
        out_specs=[pl.BlockSpec((tm, tn), lambda jj, i: (_ahead_row(jj, i), _ahead_col(jj))),
                   _side_spec(SIDE_ROWS, width, n_rows, side_blocks)],
        out_shape=[jax.ShapeDtypeStruct((m, D_FF_PAD), BF16), jax.ShapeDtypeStruct((D_FF_PAD, width), BF16)],
        scratch_shapes=[pltpu.VMEM((2, d, tn), BF16)] * 2,
        compiler_params=pltpu.CompilerParams(dimension_semantics=("arbitrary", "arbitrary"), vmem_limit_bytes=limit),
        name="ffn_up",
    )(h_bf, w_gate, w_up, w_side)


DOWN_COL_CHUNK = 1024
DOWN_RES_COLS = 512


def _ffn_down_ln_kernel(hid_ref, w_ref, h_ref, g_ref, b_ref, o_ref, *, nk, n_res, last_rows):
    k = pl.program_id(1)

    def accumulate(kk, first):
        hid = hid_ref[:, 0:kk]
        for n0 in range(0, o_ref.shape[1], DOWN_COL_CHUNK):
            part = _dot(hid, w_ref[0:kk, n0:n0 + DOWN_COL_CHUNK])
            if first:
                o_ref[:, n0:n0 + DOWN_COL_CHUNK] = part
            else:
                o_ref[:, n0:n0 + DOWN_COL_CHUNK] += part

    pl.when(k == 0)(functools.partial(accumulate, hid_ref.shape[1], True))
    pl.when(jnp.logical_and(k > 0, k < nk - 1))(functools.partial(accumulate, hid_ref.shape[1], False))
    pl.when(k == nk - 1)(functools.partial(accumulate, last_rows, False))

    @pl.when(k < n_res)
    def _():
        cols = pl.ds(pl.multiple_of(k * DOWN_RES_COLS, DOWN_RES_COLS), DOWN_RES_COLS)
        o_ref[:, cols] += DEEPNORM_ALPHA * h_ref[...]

    @pl.when(k == nk - 1)
    def _():
        _layer_norm_block(o_ref, g_ref, b_ref, (o_ref,))


def _ffn_down_ln(hid, w_down_bf, h, ln_g, ln_b):
    m, d = h.shape
    tm, tk = 1024, 512
    nk = D_FF_PAD // tk
    n_res = d // DOWN_RES_COLS
    assert n_res <= nk
    kern = functools.partial(_ffn_down_ln_kernel, nk=nk, n_res=n_res, last_rows=D_FF - (nk - 1) * tk)
    limit = _vmem_limit([((tm, tk), BF16), ((tk, d), BF16), ((tm, DOWN_RES_COLS), F32), ((tm, d), F32)],
                        [((tm, DOWN_COL_CHUNK), F32)])
    return pl.pallas_call(
        kern,
        grid=(m // tm, nk),
        in_specs=[pl.BlockSpec((tm, tk), lambda i, k: (i, k)),
                  pl.BlockSpec((tk, d), lambda i, k: (k, 0)),
                  pl.BlockSpec((tm, DOWN_RES_COLS), lambda i, k: (i, jnp.minimum(k, n_res - 1))),
                  pl.BlockSpec((1, d), lambda i, k: (0, 0)),
                  pl.BlockSpec((1, d), lambda i, k: (0, 0))],
        out_specs=pl.BlockSpec((tm, d), lambda i, k: (i, 0)),
        out_shape=jax.ShapeDtypeStruct((m, d), F32),
        compiler_params=pltpu.CompilerParams(dimension_semantics=("parallel", "arbitrary"), vmem_limit_bytes=limit),
        name="ffn_down_ln",
    )(hid, w_down_bf, h, ln_g.reshape(1, d), ln_b.reshape(1, d))


def kernel(x, positions, w_in, b_glu, conv_w, conv_b, conv_ln_g, conv_ln_b, lam_q1, lam_k1, lam_q2, lam_k2,
           subln_g, w_o, ln1_g, ln1_b, w_gate, w_up, w_down, ln2_g, ln2_b):
    batch, seq, d = x.shape
    m = batch * seq
    cos_t, sin_t = _rope_tables(positions, m)
    x2 = x.reshape(m, d)
    for l in range(DEPTH):
        lambda_init = 0.8 - 0.6 * math.exp(-0.3 * l)
        x_bf = x2.astype(BF16)
        qkv, w_o_bf = _in_proj_qkv(x_bf, w_in[l], cos_t, sin_t, w_o[l])
        glu = _in_proj_glu(x_bf, w_in[l], b_glu[l])
        conv = _conv_ln_swish(glu.reshape(batch, seq, CONV_WIDTH), conv_w[l], conv_b[l], conv_ln_g[l], conv_ln_b[l])
        attn = _diff_attention(qkv, lam_q1[l], lam_k1[l], lam_q2[l], lam_k2[l], subln_g[l], batch, seq, lambda_init)
        h, h_bf = _out_proj_ln(attn, conv.reshape(m, CONV_WIDTH), w_o_bf, x2, ln1_g[l], ln1_b[l])
        hid, w_down_bf = _ffn_up(h_bf, w_gate[l], w_up[l], w_down[l])
        x2 = _ffn_down_ln(hid, w_down_bf, h, ln2_g[l], ln2_b[l])
    return x2.reshape(batch, seq, d)
```

```python
import functools
import math

import jax
import jax.numpy as jnp
from jax import lax
from jax.experimental import pallas as pl
from jax.experimental.pallas import tpu as pltpu

D_MODEL = 4096
DEPTH = 1
ATTN_WIDTH = D_MODEL // 2
CONV_WIDTH = D_MODEL - ATTN_WIDTH
HEAD_DIM = 128
V_HEAD_DIM = 2 * HEAD_DIM
N_DIFF_HEADS = ATTN_WIDTH // V_HEAD_DIM
CONV_TAPS = 31
ROPE_THETA = 500000.0
ROPE_DIM = HEAD_DIM // 4
ROPE_HALF = ROPE_DIM // 2
D_FF = -(-8 * D_MODEL // (3 * 256)) * 256
Q_COLS = N_DIFF_HEADS * 2 * HEAD_DIM
K_COLS = Q_COLS
V_COLS = N_DIFF_HEADS * V_HEAD_DIM
QKV_COLS = Q_COLS + K_COLS + V_COLS
GLU_COLS = 2 * CONV_WIDTH
DEEPNORM_ALPHA = (2 * DEPTH) ** 0.25
LN_EPS = 1e-5

V7X_LANES = 128
V7X_SUBLANES = 8
V7X_VMEM_BYTES = 64 * 1024 * 1024
V7X_VMEM_REQUEST_CAP = 60000 * 1024

F32 = jnp.float32
BF16 = jnp.bfloat16
NEG_BIG = -1e30


def _nbytes(shape, dtype):
    return math.prod(shape) * jnp.dtype(dtype).itemsize


def _vmem_limit(pipelined, resident=()):
    need = 2 * sum(_nbytes(s, d) for s, d in pipelined) + sum(_nbytes(s, d) for s, d in resident)
    return min(max(need + need // 4, 16 * 1024 * 1024), V7X_VMEM_REQUEST_CAP)


def _dot(a, b):
    return jnp.dot(a, b, preferred_element_type=F32)


def _layer_norm_rows(v, g, b):
    mu = jnp.mean(v, axis=-1, keepdims=True)
    vc = v - mu
    var = jnp.mean(vc * vc, axis=-1, keepdims=True)
    return vc * lax.rsqrt(var + LN_EPS) * g + b


LN_CHUNK_ROWS = 128
LN_COL_GROUP = 256


def _layer_norm_block(src_ref, g_ref, b_ref, dst_refs):
    width = src_ref.shape[1]
    groups = [slice(c0, c0 + LN_COL_GROUP) for c0 in range(0, width, LN_COL_GROUP)]

    def row_sums(rows, term):
        acc = term(src_ref[rows, groups[0]])
        for cols in groups[1:]:
            acc = acc + term(src_ref[rows, cols])
        return jnp.sum(acc, axis=-1, keepdims=True)

    def chunk(c, carry):
        rows = pl.ds(pl.multiple_of(c * LN_CHUNK_ROWS, LN_CHUNK_ROWS), LN_CHUNK_ROWS)
        mu = row_sums(rows, lambda v: v) * (1.0 / width)
        var = row_sums(rows, lambda v: (v - mu) * (v - mu)) * (1.0 / width)
        rstd = lax.rsqrt(var + LN_EPS)
        for cols in groups:
            y = (src_ref[rows, cols] - mu) * rstd * g_ref[:, cols] + b_ref[:, cols]
            for dst in dst_refs:
                dst[rows, cols] = y.astype(dst.dtype)
        return carry

    lax.fori_loop(0, src_ref.shape[0] // LN_CHUNK_ROWS, chunk, 0)


def _rope_table_kernel(pos_ref, inv_ref, cos_ref, sin_ref):
    ang = pos_ref[...].astype(F32) * inv_ref[...]
    lane = lax.broadcasted_iota(jnp.int32, ang.shape, 1)
    s = jnp.sin(ang)
    cos_ref[...] = jnp.cos(ang)
    sin_ref[...] = jnp.where(lane < ROPE_HALF, -s, s)


def _rope_tables(positions, m):
    tm = 1024
    inv_freq = ROPE_THETA ** (-jnp.arange(0, ROPE_DIM, 2, dtype=F32) / ROPE_DIM)
    inv_lanes = jnp.concatenate([inv_freq, inv_freq, jnp.zeros((V7X_LANES - ROPE_DIM,), F32)]).reshape(1, V7X_LANES)
    out = jax.ShapeDtypeStruct((m, V7X_LANES), F32)
    return pl.pallas_call(
        _rope_table_kernel,
        grid=(m // tm,),
        in_specs=[pl.BlockSpec((tm, 1), lambda i: (i, 0)),
                  pl.BlockSpec((1, V7X_LANES), lambda i: (0, 0))],
        out_specs=[pl.BlockSpec((tm, V7X_LANES), lambda i: (i, 0))] * 2,
        out_shape=[out, out],
        name="rope_tables",
    )(positions.reshape(m, 1), inv_lanes)


def _ahead_row(jj, i):
    return jnp.where(jj == 0, 0, i)


def _ahead_col(jj):
    return jnp.maximum(jj - 1, 0)


def _ahead_chunk_spec(chunk, tn, n_tiles, n_rows, col0=0):
    return pl.BlockSpec((chunk, tn), lambda jj, i: (jnp.where(jj == n_tiles, n_rows - 1, i),
                                                    col0 + jnp.minimum(jj, n_tiles - 1)))


def _ahead_convert(jj, i, chunk_refs, slot_refs):
    chunk = chunk_refs[0].shape[0]
    rows = pl.ds(pl.multiple_of(i * chunk, chunk), chunk)
    for src, dst in zip(chunk_refs, slot_refs):
        dst[jj % 2, rows, :] = src[...].astype(BF16)


def _side_spec(rows, width, n_rows, n_blocks):
    return pl.BlockSpec((rows, width), lambda jj, i: (jnp.minimum(jj * n_rows + i, n_blocks - 1), 0))


SIDE_ROWS = 64


def _qkv_kernel(x_ref, wchunk_ref, cos_ref, sin_ref, side_ref, o_ref, side_o_ref, wbf_ref, *, tn, n_rot_tiles):
    jj = pl.program_id(0)
    i = pl.program_id(1)

    def convert():
        side_o_ref[...] = side_ref[...].astype(side_o_ref.dtype)
        _ahead_convert(jj, i, (wchunk_ref,), (wbf_ref,))

    pl.when(jj == 0)(convert)

    @pl.when(jj > 0)
    def _():
        convert()
        acc = _dot(x_ref[...], wbf_ref[(jj - 1) % 2])

        @pl.when(jj - 1 < n_rot_tiles)
        def _():
            c = cos_ref[...]
            s = sin_ref[...]
            lane = lax.broadcasted_iota(jnp.int32, c.shape, 1)
            for g in range(tn // V7X_LANES):
                t = acc[:, g * V7X_LANES:(g + 1) * V7X_LANES]
                partner = jnp.where(lane < ROPE_HALF,
                                    pltpu.roll(t, V7X_LANES - ROPE_HALF, 1),
                                    pltpu.roll(t, ROPE_HALF, 1))
                o_ref[:, g * V7X_LANES:(g + 1) * V7X_LANES] = (t * c + partner * s).astype(o_ref.dtype)

        @pl.when(jj - 1 >= n_rot_tiles)
        def _():
            o_ref[...] = acc.astype(o_ref.dtype)


def _in_proj_qkv(x_bf, w_in, cos_t, sin_t, w_side):
    m, d = x_bf.shape
    tm, tn = 1024, 1024
    n_rows, n_tiles = m // tm, QKV_COLS // tn
    chunk = d // n_rows
    side_rows = 2 * SIDE_ROWS
    side_blocks = w_side.shape[0] // side_rows
    assert side_blocks <= (n_tiles + 1) * n_rows
    kern = functools.partial(_qkv_kernel, tn=tn, n_rot_tiles=(Q_COLS + K_COLS) // tn)
    row_map = lambda jj, i: (_ahead_row(jj, i), 0)
    side = _side_spec(side_rows, w_side.shape[1], n_rows, side_blocks)
    limit = _vmem_limit([((tm, d), BF16), ((chunk, tn), F32), ((tm, V7X_LANES), F32), ((tm, V7X_LANES), F32),
                         ((side_rows, w_side.shape[1]), F32), ((tm, tn), BF16), ((side_rows, w_side.shape[1]), BF16)],
                        [((2, d, tn), BF16), ((tm, tn), F32)])
    return pl.pallas_call(
        kern,
        grid=(n_tiles + 1, n_rows),
        in_specs=[pl.BlockSpec((tm, d), row_map),
                  _ahead_chunk_spec(chunk, tn, n_tiles, n_rows),
                  pl.BlockSpec((tm, V7X_LANES), row_map),
                  pl.BlockSpec((tm, V7X_LANES), row_map),
                  side],
        out_specs=[pl.BlockSpec((tm, tn), lambda jj, i: (_ahead_row(jj, i), _ahead_col(jj))), side],
        out_shape=[jax.ShapeDtypeStruct((m, QKV_COLS), BF16), jax.ShapeDtypeStruct(w_side.shape, BF16)],
        scratch_shapes=[pltpu.VMEM((2, d, tn), BF16)],
        compiler_params=pltpu.CompilerParams(dimension_semantics=("arbitrary", "arbitrary"), vmem_limit_bytes=limit),
        name="in_proj_qkv",
    )(x_bf, w_in, cos_t, sin_t, w_side)


def _glu_kernel(x_ref, wa_chunk_ref, wg_chunk_ref, ba_ref, bg_ref, o_ref, wa_bf_ref, wg_bf_ref):
    jj = pl.program_id(0)
    i = pl.program_id(1)
    convert = functools.partial(_ahead_convert, jj, i, (wa_chunk_ref, wg_chunk_ref), (wa_bf_ref, wg_bf_ref))
    pl.when(jj == 0)(convert)

    @pl.when(jj > 0)
    def _():
        convert()
        x = x_ref[...]
        slot = (jj - 1) % 2
        a = _dot(x, wa_bf_ref[slot]) + ba_ref[...]
        gate = _dot(x, wg_bf_ref[slot]) + bg_ref[...]
        o_ref[...] = (a * jax.nn.sigmoid(gate)).astype(o_ref.dtype)


def _in_proj_glu(x_bf, w_in, b_glu):
    m, d = x_bf.shape
    tm, tn = 1024, 512
    n_rows, n_tiles = m // tm, CONV_WIDTH // tn
    chunk = d // n_rows
    a0 = QKV_COLS // tn
    g0 = (QKV_COLS + CONV_WIDTH) // tn
    limit = _vmem_limit([((tm, d), BF16), ((chunk, tn), F32), ((chunk, tn), F32), ((tm, tn), F32)],
                        [((2, d, tn), BF16)] * 2 + [((tm, tn), F32)] * 2)
    return pl.pallas_call(
        _glu_kernel,
        grid=(n_tiles + 1, n_rows),
        in_specs=[pl.BlockSpec((tm, d), lambda jj, i: (_ahead_row(jj, i), 0)),
                  _ahead_chunk_spec(chunk, tn, n_tiles, n_rows, a0),
                  _ahead_chunk_spec(chunk, tn, n_tiles, n_rows, g0),
                  pl.BlockSpec((1, tn), lambda jj, i: (0, _ahead_col(jj))),
                  pl.BlockSpec((1, tn), lambda jj, i: (0, n_tiles + _ahead_col(jj)))],
        out_specs=pl.BlockSpec((tm, tn), lambda jj, i: (_ahead_row(jj, i), _ahead_col(jj))),
        out_shape=jax.ShapeDtypeStruct((m, CONV_WIDTH), F32),
        scratch_shapes=[pltpu.VMEM((2, d, tn), BF16)] * 2,
        compiler_params=pltpu.CompilerParams(dimension_semantics=("arbitrary", "arbitrary"), vmem_limit_bytes=limit),
        name="in_proj_glu",
    )(x_bf, w_in, w_in, b_glu.reshape(1, GLU_COLS), b_glu.reshape(1, GLU_COLS))


CONV_HALO = 32
CONV_ROWS = 128
CONV_BACK = V7X_SUBLANES * ((CONV_TAPS - 1) // V7X_SUBLANES)


def _conv_ln_kernel(cur_ref, prev_ref, w_ref, cb_ref, g_ref, b_ref, o_ref, ext_ref, y_ref, *, ts):
    i = pl.program_id(1)
    width = cur_ref.shape[-1]
    ext_ref[0:CONV_HALO, :] = jnp.where(i > 0, prev_ref[0], 0.0)
    ext_ref[CONV_HALO:, :] = cur_ref[0]

    def channel_group(cg, carry):
        c0 = pl.multiple_of(cg * V7X_LANES, V7X_LANES)
        lanes = pl.ds(c0, V7X_LANES)
        for r0 in range(0, ts, CONV_ROWS):
            acc = jnp.zeros((CONV_ROWS, V7X_LANES), F32)
            win = ext_ref[pl.ds(r0, CONV_ROWS + CONV_HALO), lanes]
            for r in range(V7X_SUBLANES):
                shifted = pltpu.roll(win, r, 0) if r else win
                for q in range(CONV_BACK // V7X_SUBLANES + 1):
                    u = V7X_SUBLANES * q + r
                    if u < CONV_TAPS:
                        lo = CONV_HALO - V7X_SUBLANES * q
                        acc = acc + shifted[lo:lo + CONV_ROWS, :] * w_ref[pl.ds(CONV_TAPS - 1 - u, 1), lanes]
            y_ref[pl.ds(r0, CONV_ROWS), lanes] = acc + cb_ref[:, lanes]
        return carry

    lax.fori_loop(0, width // V7X_LANES, channel_group, 0)
    z = _layer_norm_rows(y_ref[...], g_ref[...], b_ref[...])
    o_ref[0] = (z * jax.nn.sigmoid(z)).astype(o_ref.dtype)


def _conv_ln_swish(c, conv_w, conv_b, ln_g, ln_b):
    b, s, width = c.shape
    ts = 256
    per = ts // CONV_HALO
    kern = functools.partial(_conv_ln_kernel, ts=ts)
    row = lambda a: a.reshape(1, width)
    limit = _vmem_limit([((ts, width), F32), ((CONV_HALO, width), F32), ((CONV_TAPS, width), F32), ((ts, width), BF16)],
                        [((ts + CONV_HALO, width), F32), ((ts, width), F32), ((ts, width), F32)])
    return pl.pallas_call(
        kern,
        grid=(b, s // ts),
        in_specs=[pl.BlockSpec((1, ts, width), lambda bi, i: (bi, i, 0)),
                  pl.BlockSpec((1, CONV_HALO, width), lambda bi, i: (bi, jnp.maximum(i * per - 1, 0), 0)),
                  pl.BlockSpec((CONV_TAPS, width), lambda bi, i: (0, 0)),
                  pl.BlockSpec((1, width), lambda bi, i: (0, 0)),
                  pl.BlockSpec((1, width), lambda bi, i: (0, 0)),
                  pl.BlockSpec((1, width), lambda bi, i: (0, 0))],
        out_specs=pl.BlockSpec((1, ts, width), lambda bi, i: (bi, i, 0)),
        out_shape=jax.ShapeDtypeStruct((b, s, width), BF16),
        scratch_shapes=[pltpu.VMEM((ts + CONV_HALO, width), F32), pltpu.VMEM((ts, width), F32)],
        compiler_params=pltpu.CompilerParams(dimension_semantics=("parallel", "arbitrary"), vmem_limit_bytes=limit),
        name="conv_ln_swish",
    )(c, c, conv_w, row(conv_b), row(ln_g), row(ln_b))


ATTN_BLOCK = 512
ATTN_HEADS_PER_STEP = 2
SCORE_SCALE_LOG2 = HEAD_DIM ** -0.5 * math.log2(math.e)


def _dot_nt(a, b):
    return lax.dot_general(a, b, (((1,), (1,)), ((), ())), preferred_element_type=F32)


def _attn_kernel(q_ref, k_ref, v_ref, lq1_ref, lk1_ref, lq2_ref, lk2_ref, g_ref, o_ref, *, seq, lambda_init):
    blk = ATTN_BLOCK
    lam = (jnp.exp(jnp.sum(lq1_ref[...] * lk1_ref[...], axis=-1, keepdims=True))
           - jnp.exp(jnp.sum(lq2_ref[...] * lk2_ref[...], axis=-1, keepdims=True)) + lambda_init)
    row = lax.broadcasted_iota(jnp.int32, (blk, blk), 0)
    col = lax.broadcasted_iota(jnp.int32, (blk, blk), 1)
    causal = col <= row

    def softmax_times_v(qk_cols, v_cols, r0):
        q = q_ref[r0:r0 + blk, qk_cols]
        s_diag = jnp.where(causal, _dot_nt(q, k_ref[r0:r0 + blk, qk_cols]), NEG_BIG)
        m = jnp.max(s_diag, axis=-1, keepdims=True)
        if r0:
            s_past = _dot_nt(q, k_ref[0:r0, qk_cols])
            m = jnp.maximum(m, jnp.max(s_past, axis=-1, keepdims=True))
        p = jnp.exp2((s_diag - m) * SCORE_SCALE_LOG2)
        l = jnp.sum(p, axis=-1, keepdims=True)
        acc = _dot(p.astype(BF16), v_ref[r0:r0 + blk, v_cols])
        if r0:
            p = jnp.exp2((s_past - m) * SCORE_SCALE_LOG2)
            l = l + jnp.sum(p, axis=-1, keepdims=True)
            acc = acc + _dot(p.astype(BF16), v_ref[0:r0, v_cols])
        return acc / l

    for r0 in reversed(range(0, seq, blk)):
        for hh in range(ATTN_HEADS_PER_STEP):
            c1 = slice(2 * HEAD_DIM * hh, 2 * HEAD_DIM * hh + HEAD_DIM)
            c2 = slice(2 * HEAD_DIM * hh + HEAD_DIM, 2 * HEAD_DIM * (hh + 1))
            vc = slice(V_HEAD_DIM * hh, V_HEAD_DIM * (hh + 1))
            o = softmax_times_v(c1, vc, r0) - lam * softmax_times_v(c2, vc, r0)
            y = o * lax.rsqrt(jnp.mean(o * o, axis=-1, keepdims=True) + LN_EPS)
            o_ref[r0:r0 + blk, vc] = ((y * g_ref[...]) * (1.0 - lambda_init)).astype(o_ref.dtype)


def _diff_attention(qkv, lam_q1, lam_k1, lam_q2, lam_k2, subln_g, batch, seq, lambda_init):
    m = qkv.shape[0]
    hps = ATTN_HEADS_PER_STEP
    kern = functools.partial(_attn_kernel, seq=seq, lambda_init=lambda_init)
    qk_w = hps * 2 * HEAD_DIM
    v_w = hps * V_HEAD_DIM
    kcol = Q_COLS // qk_w
    vcol = (Q_COLS + K_COLS) // v_w
    vec = lambda a: a.reshape(1, -1)
    vec_spec = lambda n: pl.BlockSpec((1, n), lambda b, h: (0, 0))
    limit = _vmem_limit([((seq, qk_w), BF16)] * 2 + [((seq, v_w), BF16)] * 2,
                        ([((ATTN_BLOCK, seq), F32)] * 3 + [((ATTN_BLOCK, seq), BF16)] * 2) * hps)
    return pl.pallas_call(
        kern,
        grid=(batch, N_DIFF_HEADS // hps),
        in_specs=[pl.BlockSpec((seq, qk_w), lambda b, h: (b, h)),
                  pl.BlockSpec((seq, qk_w), lambda b, h: (b, kcol + h)),
                  pl.BlockSpec((seq, v_w), lambda b, h: (b, vcol + h)),
                  vec_spec(HEAD_DIM), vec_spec(HEAD_DIM), vec_spec(HEAD_DIM), vec_spec(HEAD_DIM),
                  vec_spec(V_HEAD_DIM)],
        out_specs=pl.BlockSpec((seq, v_w), lambda b, h: (b, h)),
        out_shape=jax.ShapeDtypeStruct((m, ATTN_WIDTH), BF16),
        compiler_params=pltpu.CompilerParams(dimension_semantics=("parallel", "parallel"), vmem_limit_bytes=limit),
        name="diff_attention",
    )(qkv, qkv, qkv, vec(lam_q1), vec(lam_k1), vec(lam_q2), vec(lam_k2), vec(subln_g))


def _oproj_ln_kernel(a_ref, c_ref, wa_ref, wc_ref, x_ref, g_ref, b_ref, h_ref, hb_ref, *, tn, nj):
    j = pl.program_id(1)
    y = _dot(a_ref[...], wa_ref[...]) + _dot(c_ref[...], wc_ref[...]) + DEEPNORM_ALPHA * x_ref[...]
    h_ref[:, pl.ds(pl.multiple_of(j * tn, tn), tn)] = y

    @pl.when(j == nj - 1)
    def _():
        _layer_norm_block(h_ref, g_ref, b_ref, (h_ref, hb_ref))


def _out_proj_ln(attn, conv, w_o_bf, x2, ln_g, ln_b):
    m, d = x2.shape
    tm, tn = 512, 1024
    nj = d // tn
    kern = functools.partial(_oproj_ln_kernel, tn=tn, nj=nj)
    limit = _vmem_limit([((tm, ATTN_WIDTH), BF16), ((tm, CONV_WIDTH), BF16), ((ATTN_WIDTH, tn), BF16),
                         ((CONV_WIDTH, tn), BF16), ((tm, tn), F32), ((tm, d), F32), ((tm, d), BF16)],
                        [((tm, tn), F32)] * 2)
    return pl.pallas_call(
        kern,
        grid=(m // tm, nj),
        in_specs=[pl.BlockSpec((tm, ATTN_WIDTH), lambda i, j: (i, 0)),
                  pl.BlockSpec((tm, CONV_WIDTH), lambda i, j: (i, 0)),
                  pl.BlockSpec((ATTN_WIDTH, tn), lambda i, j: (0, j)),
                  pl.BlockSpec((CONV_WIDTH, tn), lambda i, j: (1, j)),
                  pl.BlockSpec((tm, tn), lambda i, j: (i, j)),
                  pl.BlockSpec((1, d), lambda i, j: (0, 0)),
                  pl.BlockSpec((1, d), lambda i, j: (0, 0))],
        out_specs=[pl.BlockSpec((tm, d), lambda i, j: (i, 0)),
                   pl.BlockSpec((tm, d), lambda i, j: (i, 0))],
        out_shape=[jax.ShapeDtypeStruct((m, d), F32), jax.ShapeDtypeStruct((m, d), BF16)],
        compiler_params=pltpu.CompilerParams(dimension_semantics=("parallel", "arbitrary"), vmem_limit_bytes=limit),
        name="out_proj_ln",
    )(attn, conv, w_o_bf, w_o_bf, x2, ln_g.reshape(1, d), ln_b.reshape(1, d))


FF_TILE = 512
D_FF_PAD = -(-D_FF // FF_TILE) * FF_TILE


def _ffn_up_kernel(h_ref, wg_chunk_ref, wu_chunk_ref, side_ref, o_ref, side_o_ref, wg_bf_ref, wu_bf_ref, *,
                   tn, n_tiles, last_cols, n_rows, side_valid_blocks):
    jj = pl.program_id(0)
    i = pl.program_id(1)

    def convert():
        side_o_ref[...] = jnp.where(jj * n_rows + i < side_valid_blocks, side_ref[...], 0.0).astype(side_o_ref.dtype)
        _ahead_convert(jj, i, (wg_chunk_ref, wu_chunk_ref), (wg_bf_ref, wu_bf_ref))

    pl.when(jj == 0)(convert)

    def tile(ncols):
        convert()
        h = h_ref[...]
        slot = (jj - 1) % 2
        g = _dot(h, wg_bf_ref[slot, :, 0:ncols])
        u = _dot(h, wu_bf_ref[slot, :, 0:ncols])
        o_ref[:, 0:ncols] = ((g * jax.nn.sigmoid(g)) * u).astype(o_ref.dtype)
        if ncols < tn:
            o_ref[:, ncols:] = jnp.zeros((o_ref.shape[0], tn - ncols), o_ref.dtype)

    pl.when(jnp.logical_and(jj > 0, jj < n_tiles))(functools.partial(tile, tn))
    pl.when(jj == n_tiles)(functools.partial(tile, last_cols))


def _ffn_up(h_bf, w_gate, w_up, w_side):
    m, d = h_bf.shape
    tm, tn = 1024, FF_TILE
    n_rows, n_tiles = m // tm, D_FF_PAD // tn
    chunk = d // n_rows
    side_valid_blocks = w_side.shape[0] // SIDE_ROWS
    side_blocks = D_FF_PAD // SIDE_ROWS
    assert side_blocks <= (n_tiles + 1) * n_rows and side_valid_blocks * SIDE_ROWS == w_side.shape[0]
    kern = functools.partial(_ffn_up_kernel, tn=tn, n_tiles=n_tiles, last_cols=D_FF - (n_tiles - 1) * tn,
                             n_rows=n_rows, side_valid_blocks=side_valid_blocks)
    width = w_side.shape[1]
    limit = _vmem_limit([((tm, d), BF16), ((chunk, tn), F32), ((chunk, tn), F32), ((SIDE_ROWS, width), F32),
                         ((tm, tn), BF16), ((SIDE_ROWS, width), BF16)],
                        [((2, d, tn), BF16)] * 2 + [((tm, tn), F32)] * 3)
    return pl.pallas_call(
        kern,
        grid=(n_tiles + 1, n_rows),
        in_specs=[pl.BlockSpec((tm, d), lambda jj, i: (_ahead_row(jj, i), 0)),
                  _ahead_chunk_spec(chunk, tn, n_tiles, n_rows),
                  _ahead_chunk_spec(chunk, tn, n_tiles, n_rows),
                  _side_spec(SIDE_ROWS, width, n_rows, side_valid_blocks)],
        out_specs=[pl.BlockSpec((tm, tn), lambda jj, i: (_ahead_row(jj, i), _ahead_col(jj))),
                   _side_spec(SIDE_ROWS, width, n_rows, side_blocks)],
        out_shape=[jax.ShapeDtypeStruct((m, D_FF_PAD), BF16), jax.ShapeDtypeStruct((D_FF_PAD, width), BF16)],
        scratch_shapes=[pltpu.VMEM((2, d, tn), BF16)] * 2,
        compiler_params=pltpu.CompilerParams(dimension_semantics=("arbitrary", "arbitrary"), vmem_limit_bytes=limit),
        name="ffn_up",
    )(h_bf, w_gate, w_up, w_side)


DOWN_COL_CHUNK = 1024
DOWN_RES_COLS = 512


def _ffn_down_ln_kernel(hid_ref, w_ref, h_ref, g_ref, b_ref, o_ref, *, nk, n_res, last_rows):
    k = pl.program_id(1)

    def accumulate(kk, first):
        hid = hid_ref[:, 0:kk]
        for n0 in range(0, o_ref.shape[1], DOWN_COL_CHUNK):
            part = _dot(hid, w_ref[0:kk, n0:n0 + DOWN_COL_CHUNK])
            if first:
                o_ref[:, n0:n0 + DOWN_COL_CHUNK] = part
            else:
                o_ref[:, n0:n0 + DOWN_COL_CHUNK] += part

    pl.when(k == 0)(functools.partial(accumulate, hid_ref.shape[1], True))
    pl.when(jnp.logical_and(k > 0, k < nk - 1))(functools.partial(accumulate, hid_ref.shape[1], False))
    pl.when(k == nk - 1)(functools.partial(accumulate, last_rows, False))

    @pl.when(k < n_res)
    def _():
        cols = pl.ds(pl.multiple_of(k * DOWN_RES_COLS, DOWN_RES_COLS), DOWN_RES_COLS)
        o_ref[:, cols] += DEEPNORM_ALPHA * h_ref[...]

    @pl.when(k == nk - 1)
    def _():
        _layer_norm_block(o_ref, g_ref, b_ref, (o_ref,))


def _ffn_down_ln(hid, w_down_bf, h, ln_g, ln_b):
    m, d = h.shape
    tm, tk = 1024, 512
    nk = D_FF_PAD // tk
    n_res = d // DOWN_RES_COLS
    assert n_res <= nk
    kern = functools.partial(_ffn_down_ln_kernel, nk=nk, n_res=n_res, last_rows=D_FF - (nk - 1) * tk)
    limit = _vmem_limit([((tm, tk), BF16), ((tk, d), BF16), ((tm, DOWN_RES_COLS), F32), ((tm, d), F32)],
                        [((tm, DOWN_COL_CHUNK), F32)])
    return pl.pallas_call(
        kern,
        grid=(m // tm, nk),
        in_specs=[pl.BlockSpec((tm, tk), lambda i, k: (i, k)),
                  pl.BlockSpec((tk, d), lambda i, k: (k, 0)),
                  pl.BlockSpec((tm, DOWN_RES_COLS), lambda i, k: (i, jnp.minimum(k, n_res - 1))),
                  pl.BlockSpec((1, d), lambda i, k: (0, 0)),
                  pl.BlockSpec((1, d), lambda i, k: (0, 0))],
        out_specs=pl.BlockSpec((tm, d), lambda i, k: (i, 0)),
        out_shape=jax.ShapeDtypeStruct((m, d), F32),
        compiler_params=pltpu.CompilerParams(dimension_semantics=("parallel", "arbitrary"), vmem_limit_bytes=limit),
        name="ffn_down_ln",
    )(hid, w_down_bf, h, ln_g.reshape(1, d), ln_b.reshape(1, d))


def kernel(x, positions, w_in, b_glu, conv_w, conv_b, conv_ln_g, conv_ln_b, lam_q1, lam_k1, lam_q2, lam_k2,
           subln_g, w_o, ln1_g, ln1_b, w_gate, w_up, w_down, ln2_g, ln2_b):
    batch, seq, d = x.shape
    m = batch * seq
    cos_t, sin_t = _rope_tables(positions, m)
    x2 = x.reshape(m, d)
    for l in range(DEPTH):
        lambda_init = 0.8 - 0.6 * math.exp(-0.3 * l)
        x_bf = x2.astype(BF16)
        qkv, w_o_bf = _in_proj_qkv(x_bf, w_in[l], cos_t, sin_t, w_o[l])
        glu = _in_proj_glu(x_bf, w_in[l], b_glu[l])
        conv = _conv_ln_swish(glu.reshape(batch, seq, CONV_WIDTH), conv_w[l], conv_b[l], conv_ln_g[l], conv_ln_b[l])
        attn = _diff_attention(qkv, lam_q1[l], lam_k1[l], lam_q2[l], lam_k2[l], subln_g[l], batch, seq, lambda_init)
        h, h_bf = _out_proj_ln(attn, conv.reshape(m, CONV_WIDTH), w_o_bf, x2, ln1_g[l], ln1_b[l])
        hid, w_down_bf = _ffn_up(h_bf, w_gate[l], w_up[l], w_down[l])
        x2 = _ffn_down_ln(hid, w_down_bf, h, ln2_g[l], ln2_b[l])
    return x2.reshape(batch, seq, d)
```

```python
import functools
import math

import jax
import jax.numpy as jnp
from jax import lax
from jax.experimental import pallas as pl
from jax.experimental.pallas import tpu as pltpu

D_MODEL = 4096
DEPTH = 1
ATTN_WIDTH = D_MODEL // 2
CONV_WIDTH = D_MODEL - ATTN_WIDTH
HEAD_DIM = 128
V_HEAD_DIM = 2 * HEAD_DIM
N_DIFF_HEADS = ATTN_WIDTH // V_HEAD_DIM
CONV_TAPS = 31
ROPE_THETA = 500000.0
ROPE_DIM = HEAD_DIM // 4
ROPE_HALF = ROPE_DIM // 2
D_FF = -(-8 * D_MODEL // (3 * 256)) * 256
Q_COLS = N_DIFF_HEADS * 2 * HEAD_DIM
K_COLS = Q_COLS
V_COLS = N_DIFF_HEADS * V_HEAD_DIM
QKV_COLS = Q_COLS + K_COLS + V_COLS
GLU_COLS = 2 * CONV_WIDTH
DEEPNORM_ALPHA = (2 * DEPTH) ** 0.25
LN_EPS = 1e-5

V7X_LANES = 128
V7X_SUBLANES = 8
V7X_VMEM_BYTES = 64 * 1024 * 1024
V7X_VMEM_REQUEST_CAP = 60000 * 1024

F32 = jnp.float32
BF16 = jnp.bfloat16
NEG_BIG = -1e30


def _nbytes(shape, dtype):
    return math.prod(shape) * jnp.dtype(dtype).itemsize


def _vmem_limit(pipelined, resident=()):
    need = 2 * sum(_nbytes(s, d) for s, d in pipelined) + sum(_nbytes(s, d) for s, d in resident)
    return min(max(need + need // 4, 16 * 1024 * 1024), V7X_VMEM_REQUEST_CAP)


def _dot(a, b):
    return jnp.dot(a, b, preferred_element_type=F32)


def _layer_norm_rows(v, g, b):
    mu = jnp.mean(v, axis=-1, keepdims=True)
    vc = v - mu
    var = jnp.mean(vc * vc, axis=-1, keepdims=True)
    return vc * lax.rsqrt(var + LN_EPS) * g + b


LN_CHUNK_ROWS = 128
LN_COL_GROUP = 256


def _layer_norm_block(src_ref, g_ref, b_ref, dst_refs):
    width = src_ref.shape[1]
    groups = [slice(c0, c0 + LN_COL_GROUP) for c0 in range(0, width, LN_COL_GROUP)]

    def row_sums(rows, term):
        acc = term(src_ref[rows, groups[0]])
        for cols in groups[1:]:
            acc = acc + term(src_ref[rows, cols])
        return jnp.sum(acc, axis=-1, keepdims=True)

    def chunk(c, carry):
        rows = pl.ds(pl.multiple_of(c * LN_CHUNK_ROWS, LN_CHUNK_ROWS), LN_CHUNK_ROWS)
        mu = row_sums(rows, lambda v: v) * (1.0 / width)
        var = row_sums(rows, lambda v: (v - mu) * (v - mu)) * (1.0 / width)
        rstd = lax.rsqrt(var + LN_EPS)
        for cols in groups:
            y = (src_ref[rows, cols] - mu) * rstd * g_ref[:, cols] + b_ref[:, cols]
            for dst in dst_refs:
                dst[rows, cols] = y.astype(dst.dtype)
        return carry

    lax.fori_loop(0, src_ref.shape[0] // LN_CHUNK_ROWS, chunk, 0)


def _rope_table_kernel(pos_ref, inv_ref, cos_ref, sin_ref):
    ang = pos_ref[...].astype(F32) * inv_ref[...]
    lane = lax.broadcasted_iota(jnp.int32, ang.shape, 1)
    s = jnp.sin(ang)
    cos_ref[...] = jnp.cos(ang)
    sin_ref[...] = jnp.where(lane < ROPE_HALF, -s, s)


def _rope_and_cast_kernel(pos_ref, inv_ref, x_ref, cos_ref, sin_ref, xb_ref):
    _rope_table_kernel(pos_ref, inv_ref, cos_ref, sin_ref)
    xb_ref[...] = x_ref[...].astype(xb_ref.dtype)


def _rope_tables_and_cast(positions, x2):
    m, d = x2.shape
    tm = 256
    inv_freq = ROPE_THETA ** (-jnp.arange(0, ROPE_DIM, 2, dtype=F32) / ROPE_DIM)
    inv_lanes = jnp.concatenate([inv_freq, inv_freq, jnp.zeros((V7X_LANES - ROPE_DIM,), F32)]).reshape(1, V7X_LANES)
    table = jax.ShapeDtypeStruct((m, V7X_LANES), F32)
    limit = _vmem_limit([((tm, d), F32), ((tm, d), BF16), ((tm, V7X_LANES), F32), ((tm, V7X_LANES), F32),
                         ((tm, V7X_LANES), F32)])
    return pl.pallas_call(
        _rope_and_cast_kernel,
        grid=(m // tm,),
        in_specs=[pl.BlockSpec((tm, 1), lambda i: (i, 0)),
                  pl.BlockSpec((1, V7X_LANES), lambda i: (0, 0)),
                  pl.BlockSpec((tm, d), lambda i: (i, 0))],
        out_specs=[pl.BlockSpec((tm, V7X_LANES), lambda i: (i, 0))] * 2 + [pl.BlockSpec((tm, d), lambda i: (i, 0))],
        out_shape=[table, table, jax.ShapeDtypeStruct((m, d), BF16)],
        compiler_params=pltpu.CompilerParams(dimension_semantics=("parallel",), vmem_limit_bytes=limit),
        name="rope_tables_and_cast",
    )(positions.reshape(m, 1), inv_lanes, x2)


def _ahead_row(jj, i):
    return jnp.where(jj == 0, 0, i)


def _ahead_col(jj):
    return jnp.maximum(jj - 1, 0)


def _ahead_chunk_spec(chunk, tn, n_tiles, n_rows, col0=0):
    return pl.BlockSpec((chunk, tn), lambda jj, i: (jnp.where(jj == n_tiles, n_rows - 1, i),
                                                    col0 + jnp.minimum(jj, n_tiles - 1)))


def _ahead_convert(jj, i, chunk_refs, slot_refs):
    chunk = chunk_refs[0].shape[0]
    rows = pl.ds(pl.multiple_of(i * chunk, chunk), chunk)
    for src, dst in zip(chunk_refs, slot_refs):
        dst[jj % 2, rows, :] = src[...].astype(BF16)


def _side_spec(rows, width, n_rows, n_blocks):
    return pl.BlockSpec((rows, width), lambda jj, i: (jnp.minimum(jj * n_rows + i, n_blocks - 1), 0))


SIDE_ROWS = 64


def _qkv_kernel(x_ref, wchunk_ref, cos_ref, sin_ref, side_ref, o_ref, side_o_ref, wbf_ref, *, tn, n_rot_tiles):
    jj = pl.program_id(0)
    i = pl.program_id(1)

    def convert():
        side_o_ref[...] = side_ref[...].astype(side_o_ref.dtype)
        _ahead_convert(jj, i, (wchunk_ref,), (wbf_ref,))

    pl.when(jj == 0)(convert)

    @pl.when(jj > 0)
    def _():
        convert()
        acc = _dot(x_ref[...], wbf_ref[(jj - 1) % 2])

        @pl.when(jj - 1 < n_rot_tiles)
        def _():
            c = cos_ref[...]
            s = sin_ref[...]
            lane = lax.broadcasted_iota(jnp.int32, c.shape, 1)
            for g in range(tn // V7X_LANES):
                t = acc[:, g * V7X_LANES:(g + 1) * V7X_LANES]
                partner = jnp.where(lane < ROPE_HALF,
                                    pltpu.roll(t, V7X_LANES - ROPE_HALF, 1),
                                    pltpu.roll(t, ROPE_HALF, 1))
                o_ref[:, g * V7X_LANES:(g + 1) * V7X_LANES] = (t * c + partner * s).astype(o_ref.dtype)

        @pl.when(jj - 1 >= n_rot_tiles)
        def _():
            o_ref[...] = acc.astype(o_ref.dtype)


def _in_proj_qkv(x_bf, w_in, cos_t, sin_t, w_side):
    m, d = x_bf.shape
    tm, tn = 1024, 1024
    n_rows, n_tiles = m // tm, QKV_COLS // tn
    chunk = d // n_rows
    side_rows = 2 * SIDE_ROWS
    side_blocks = w_side.shape[0] // side_rows
    assert side_blocks <= (n_tiles + 1) * n_rows
    kern = functools.partial(_qkv_kernel, tn=tn, n_rot_tiles=(Q_COLS + K_COLS) // tn)
    row_map = lambda jj, i: (_ahead_row(jj, i), 0)
    side = _side_spec(side_rows, w_side.shape[1], n_rows, side_blocks)
    limit = _vmem_limit([((tm, d), BF16), ((chunk, tn), F32), ((tm, V7X_LANES), F32), ((tm, V7X_LANES), F32),
                         ((side_rows, w_side.shape[1]), F32), ((tm, tn), BF16), ((side_rows, w_side.shape[1]), BF16)],
                        [((2, d, tn), BF16), ((tm, tn), F32)])
    return pl.pallas_call(
        kern,
        grid=(n_tiles + 1, n_rows),
        in_specs=[pl.BlockSpec((tm, d), row_map),
                  _ahead_chunk_spec(chunk, tn, n_tiles, n_rows),
                  pl.BlockSpec((tm, V7X_LANES), row_map),
                  pl.BlockSpec((tm, V7X_LANES), row_map),
                  side],
        out_specs=[pl.BlockSpec((tm, tn), lambda jj, i: (_ahead_row(jj, i), _ahead_col(jj))), side],
        out_shape=[jax.ShapeDtypeStruct((m, QKV_COLS), BF16), jax.ShapeDtypeStruct(w_side.shape, BF16)],
        scratch_shapes=[pltpu.VMEM((2, d, tn), BF16)],
        compiler_params=pltpu.CompilerParams(dimension_semantics=("arbitrary", "arbitrary"), vmem_limit_bytes=limit),
        name="in_proj_qkv",
    )(x_bf, w_in, cos_t, sin_t, w_side)


def _glu_kernel(x_ref, wa_chunk_ref, wg_chunk_ref, ba_ref, bg_ref, o_ref, wa_bf_ref, wg_bf_ref):
    jj = pl.program_id(0)
    i = pl.program_id(1)
    convert = functools.partial(_ahead_convert, jj, i, (wa_chunk_ref, wg_chunk_ref), (wa_bf_ref, wg_bf_ref))
    pl.when(jj == 0)(convert)

    @pl.when(jj > 0)
    def _():
        convert()
        x = x_ref[...]
        slot = (jj - 1) % 2
        a = _dot(x, wa_bf_ref[slot]) + ba_ref[...]
        gate = _dot(x, wg_bf_ref[slot]) + bg_ref[...]
        o_ref[...] = (a * jax.nn.sigmoid(gate)).astype(o_ref.dtype)


def _in_proj_glu(x_bf, w_in, b_glu):
    m, d = x_bf.shape
    tm, tn = 1024, 512
    n_rows, n_tiles = m // tm, CONV_WIDTH // tn
    chunk = d // n_rows
    a0 = QKV_COLS // tn
    g0 = (QKV_COLS + CONV_WIDTH) // tn
    limit = _vmem_limit([((tm, d), BF16), ((chunk, tn), F32), ((chunk, tn), F32), ((tm, tn), F32)],
                        [((2, d, tn), BF16)] * 2 + [((tm, tn), F32)] * 2)
    return pl.pallas_call(
        _glu_kernel,
        grid=(n_tiles + 1, n_rows),
        in_specs=[pl.BlockSpec((tm, d), lambda jj, i: (_ahead_row(jj, i), 0)),
                  _ahead_chunk_spec(chunk, tn, n_tiles, n_rows, a0),
                  _ahead_chunk_spec(chunk, tn, n_tiles, n_rows, g0),
                  pl.BlockSpec((1, tn), lambda jj, i: (0, _ahead_col(jj))),
                  pl.BlockSpec((1, tn), lambda jj, i: (0, n_tiles + _ahead_col(jj)))],
        out_specs=pl.BlockSpec((tm, tn), lambda jj, i: (_ahead_row(jj, i), _ahead_col(jj))),
        out_shape=jax.ShapeDtypeStruct((m, CONV_WIDTH), F32),
        scratch_shapes=[pltpu.VMEM((2, d, tn), BF16)] * 2,
        compiler_params=pltpu.CompilerParams(dimension_semantics=("arbitrary", "arbitrary"), vmem_limit_bytes=limit),
        name="in_proj_glu",
    )(x_bf, w_in, w_in, b_glu.reshape(1, GLU_COLS), b_glu.reshape(1, GLU_COLS))


CONV_HALO = 32
CONV_ROWS = 128
CONV_BACK = V7X_SUBLANES * ((CONV_TAPS - 1) // V7X_SUBLANES)


def _conv_ln_kernel(cur_ref, prev_ref, w_ref, cb_ref, g_ref, b_ref, o_ref, ext_ref, y_ref, *, ts):
    i = pl.program_id(1)
    width = cur_ref.shape[-1]
    ext_ref[0:CONV_HALO, :] = jnp.where(i > 0, prev_ref[0], 0.0)
    ext_ref[CONV_HALO:, :] = cur_ref[0]

    def channel_group(cg, carry):
        c0 = pl.multiple_of(cg * V7X_LANES, V7X_LANES)
        lanes = pl.ds(c0, V7X_LANES)
        for r0 in range(0, ts, CONV_ROWS):
            acc = jnp.zeros((CONV_ROWS, V7X_LANES), F32)
            win = ext_ref[pl.ds(r0, CONV_ROWS + CONV_HALO), lanes]
            for r in range(V7X_SUBLANES):
                shifted = pltpu.roll(win, r, 0) if r else win
                for q in range(CONV_BACK // V7X_SUBLANES + 1):
                    u = V7X_SUBLANES * q + r
                    if u < CONV_TAPS:
                        lo = CONV_HALO - V7X_SUBLANES * q
                        acc = acc + shifted[lo:lo + CONV_ROWS, :] * w_ref[pl.ds(CONV_TAPS - 1 - u, 1), lanes]
            y_ref[pl.ds(r0, CONV_ROWS), lanes] = acc + cb_ref[:, lanes]
        return carry

    lax.fori_loop(0, width // V7X_LANES, channel_group, 0)
    z = _layer_norm_rows(y_ref[...], g_ref[...], b_ref[...])
    o_ref[0] = (z * jax.nn.sigmoid(z)).astype(o_ref.dtype)


def _conv_ln_swish(c, conv_w, conv_b, ln_g, ln_b):
    b, s, width = c.shape
    ts = 256
    per = ts // CONV_HALO
    kern = functools.partial(_conv_ln_kernel, ts=ts)
    row = lambda a: a.reshape(1, width)
    limit = _vmem_limit([((ts, width), F32), ((CONV_HALO, width), F32), ((CONV_TAPS, width), F32), ((ts, width), BF16)],
                        [((ts + CONV_HALO, width), F32), ((ts, width), F32), ((ts, width), F32)])
    return pl.pallas_call(
        kern,
        grid=(b, s // ts),
        in_specs=[pl.BlockSpec((1, ts, width), lambda bi, i: (bi, i, 0)),
                  pl.BlockSpec((1, CONV_HALO, width), lambda bi, i: (bi, jnp.maximum(i * per - 1, 0), 0)),
                  pl.BlockSpec((CONV_TAPS, width), lambda bi, i: (0, 0)),
                  pl.BlockSpec((1, width), lambda bi, i: (0, 0)),
                  pl.BlockSpec((1, width), lambda bi, i: (0, 0)),
                  pl.BlockSpec((1, width), lambda bi, i: (0, 0))],
        out_specs=pl.BlockSpec((1, ts, width), lambda bi, i: (bi, i, 0)),
        out_shape=jax.ShapeDtypeStruct((b, s, width), BF16),
        scratch_shapes=[pltpu.VMEM((ts + CONV_HALO, width), F32), pltpu.VMEM((ts, width), F32)],
        compiler_params=pltpu.CompilerParams(dimension_semantics=("parallel", "arbitrary"), vmem_limit_bytes=limit),
        name="conv_ln_swish",
    )(c, c, conv_w, row(conv_b), row(ln_g), row(ln_b))


ATTN_BLOCK = 512
ATTN_HEADS_PER_STEP = 2
SCORE_SCALE_LOG2 = HEAD_DIM ** -0.5 * math.log2(math.e)


def _dot_nt(a, b):
    return lax.dot_general(a, b, (((1,), (1,)), ((), ())), preferred_element_type=F32)


def _attn_kernel(q_ref, k_ref, v_ref, lq1_ref, lk1_ref, lq2_ref, lk2_ref, g_ref, o_ref, *, seq, lambda_init):
    blk = ATTN_BLOCK
    lam = (jnp.exp(jnp.sum(lq1_ref[...] * lk1_ref[...], axis=-1, keepdims=True))
           - jnp.exp(jnp.sum(lq2_ref[...] * lk2_ref[...], axis=-1, keepdims=True)) + lambda_init)
    row = lax.broadcasted_iota(jnp.int32, (blk, blk), 0)
    col = lax.broadcasted_iota(jnp.int32, (blk, blk), 1)
    causal = col <= row

    def softmax_times_v(qk_cols, v_cols, r0):
        q = q_ref[r0:r0 + blk, qk_cols]
        s_diag = jnp.where(causal, _dot_nt(q, k_ref[r0:r0 + blk, qk_cols]), NEG_BIG)
        m = jnp.max(s_diag, axis=-1, keepdims=True)
        if r0:
            s_past = _dot_nt(q, k_ref[0:r0, qk_cols])
            m = jnp.maximum(m, jnp.max(s_past, axis=-1, keepdims=True))
        p = jnp.exp2((s_diag - m) * SCORE_SCALE_LOG2)
        l = jnp.sum(p, axis=-1, keepdims=True)
        acc = _dot(p.astype(BF16), v_ref[r0:r0 + blk, v_cols])
        if r0:
            p = jnp.exp2((s_past - m) * SCORE_SCALE_LOG2)
            l = l + jnp.sum(p, axis=-1, keepdims=True)
            acc = acc + _dot(p.astype(BF16), v_ref[0:r0, v_cols])
        return acc / l

    for r0 in reversed(range(0, seq, blk)):
        for hh in range(ATTN_HEADS_PER_STEP):
            c1 = slice(2 * HEAD_DIM * hh, 2 * HEAD_DIM * hh + HEAD_DIM)
            c2 = slice(2 * HEAD_DIM * hh + HEAD_DIM, 2 * HEAD_DIM * (hh + 1))
            vc = slice(V_HEAD_DIM * hh, V_HEAD_DIM * (hh + 1))
            o = softmax_times_v(c1, vc, r0) - lam * softmax_times_v(c2, vc, r0)
            y = o * lax.rsqrt(jnp.mean(o * o, axis=-1, keepdims=True) + LN_EPS)
            o_ref[r0:r0 + blk, vc] = ((y * g_ref[...]) * (1.0 - lambda_init)).astype(o_ref.dtype)


def _diff_attention(qkv, lam_q1, lam_k1, lam_q2, lam_k2, subln_g, batch, seq, lambda_init):
    m = qkv.shape[0]
    hps = ATTN_HEADS_PER_STEP
    kern = functools.partial(_attn_kernel, seq=seq, lambda_init=lambda_init)
    qk_w = hps * 2 * HEAD_DIM
    v_w = hps * V_HEAD_DIM
    kcol = Q_COLS // qk_w
    vcol = (Q_COLS + K_COLS) // v_w
    vec = lambda a: a.reshape(1, -1)
    vec_spec = lambda n: pl.BlockSpec((1, n), lambda b, h: (0, 0))
    limit = _vmem_limit([((seq, qk_w), BF16)] * 2 + [((seq, v_w), BF16)] * 2,
                        ([((ATTN_BLOCK, seq), F32)] * 3 + [((ATTN_BLOCK, seq), BF16)] * 2) * hps)
    return pl.pallas_call(
        kern,
        grid=(batch, N_DIFF_HEADS // hps),
        in_specs=[pl.BlockSpec((seq, qk_w), lambda b, h: (b, h)),
                  pl.BlockSpec((seq, qk_w), lambda b, h: (b, kcol + h)),
                  pl.BlockSpec((seq, v_w), lambda b, h: (b, vcol + h)),
                  vec_spec(HEAD_DIM), vec_spec(HEAD_DIM), vec_spec(HEAD_DIM), vec_spec(HEAD_DIM),
                  vec_spec(V_HEAD_DIM)],
        out_specs=pl.BlockSpec((seq, v_w), lambda b, h: (b, h)),
        out_shape=jax.ShapeDtypeStruct((m, ATTN_WIDTH), BF16),
        compiler_params=pltpu.CompilerParams(dimension_semantics=("parallel", "parallel"), vmem_limit_bytes=limit),
        name="diff_attention",
    )(qkv, qkv, qkv, vec(lam_q1), vec(lam_k1), vec(lam_q2), vec(lam_k2), vec(subln_g))


def _oproj_ln_kernel(a_ref, c_ref, wa_ref, wc_ref, x_ref, g_ref, b_ref, h_ref, hb_ref, *, tn, nj):
    j = pl.program_id(1)
    y = _dot(a_ref[...], wa_ref[...]) + _dot(c_ref[...], wc_ref[...]) + DEEPNORM_ALPHA * x_ref[...]
    h_ref[:, pl.ds(pl.multiple_of(j * tn, tn), tn)] = y

    @pl.when(j == nj - 1)
    def _():
        _layer_norm_block(h_ref, g_ref, b_ref, (h_ref, hb_ref))


def _out_proj_ln(attn, conv, w_o_bf, x2, ln_g, ln_b):
    m, d = x2.shape
    tm, tn = 512, 1024
    nj = d // tn
    kern = functools.partial(_oproj_ln_kernel, tn=tn, nj=nj)
    limit = _vmem_limit([((tm, ATTN_WIDTH), BF16), ((tm, CONV_WIDTH), BF16), ((ATTN_WIDTH, tn), BF16),
                         ((CONV_WIDTH, tn), BF16), ((tm, tn), F32), ((tm, d), F32), ((tm, d), BF16)],
                        [((tm, tn), F32)] * 2)
    return pl.pallas_call(
        kern,
        grid=(m // tm, nj),
        in_specs=[pl.BlockSpec((tm, ATTN_WIDTH), lambda i, j: (i, 0)),
                  pl.BlockSpec((tm, CONV_WIDTH), lambda i, j: (i, 0)),
                  pl.BlockSpec((ATTN_WIDTH, tn), lambda i, j: (0, j)),
                  pl.BlockSpec((CONV_WIDTH, tn), lambda i, j: (1, j)),
                  pl.BlockSpec((tm, tn), lambda i, j: (i, j)),
                  pl.BlockSpec((1, d), lambda i, j: (0, 0)),
                  pl.BlockSpec((1, d), lambda i, j: (0, 0))],
        out_specs=[pl.BlockSpec((tm, d), lambda i, j: (i, 0)),
                   pl.BlockSpec((tm, d), lambda i, j: (i, 0))],
        out_shape=[jax.ShapeDtypeStruct((m, d), F32), jax.ShapeDtypeStruct((m, d), BF16)],
        compiler_params=pltpu.CompilerParams(dimension_semantics=("parallel", "arbitrary"), vmem_limit_bytes=limit),
        name="out_proj_ln",
    )(attn, conv, w_o_bf, w_o_bf, x2, ln_g.reshape(1, d), ln_b.reshape(1, d))


FF_TILE = 512
D_FF_PAD = -(-D_FF // FF_TILE) * FF_TILE


def _ffn_up_kernel(h_ref, wg_chunk_ref, wu_chunk_ref, side_ref, o_ref, side_o_ref, wg_bf_ref, wu_bf_ref, *,
                   tn, n_tiles, last_cols, n_rows, side_valid_blocks):
    jj = pl.program_id(0)
    i = pl.program_id(1)

    def convert():
        side_o_ref[...] = jnp.where(jj * n_rows + i < side_valid_blocks, side_ref[...], 0.0).astype(side_o_ref.dtype)
        _ahead_convert(jj, i, (wg_chunk_ref, wu_chunk_ref), (wg_bf_ref, wu_bf_ref))

    pl.when(jj == 0)(convert)

    def tile(ncols):
        convert()
        h = h_ref[...]
        slot = (jj - 1) % 2
        g = _dot(h, wg_bf_ref[slot, :, 0:ncols])
        u = _dot(h, wu_bf_ref[slot, :, 0:ncols])
        o_ref[:, 0:ncols] = ((g * jax.nn.sigmoid(g)) * u).astype(o_ref.dtype)
        if ncols < tn:
            o_ref[:, ncols:] = jnp.zeros((o_ref.shape[0], tn - ncols), o_ref.dtype)

    pl.when(jnp.logical_and(jj > 0, jj < n_tiles))(functools.partial(tile, tn))
    pl.when(jj == n_tiles)(functools.partial(tile, last_cols))


def _ffn_up(h_bf, w_gate, w_up, w_side):
    m, d = h_bf.shape
    tm, tn = 1024, FF_TILE
    n_rows, n_tiles = m // tm, D_FF_PAD // tn
    chunk = d // n_rows
    side_valid_blocks = w_side.shape[0] // SIDE_ROWS
    side_blocks = D_FF_PAD // SIDE_ROWS
    assert side_blocks <= (n_tiles + 1) * n_rows and side_valid_blocks * SIDE_ROWS == w_side.shape[0]
    kern = functools.partial(_ffn_up_kernel, tn=tn, n_tiles=n_tiles, last_cols=D_FF - (n_tiles - 1) * tn,
                             n_rows=n_rows, side_valid_blocks=side_valid_blocks)
    width = w_side.shape[1]
    limit = _vmem_limit([((tm, d), BF16), ((chunk, tn), F32), ((chunk, tn), F32), ((SIDE_ROWS, width), F32),
                         ((tm, tn), BF16), ((SIDE_ROWS, width), BF16)],
                        [((2, d, tn), BF16)] * 2 + [((tm, tn), F32)] * 3)
    return pl.pallas_call(
        kern,
        grid=(n_tiles + 1, n_rows),
        in_specs=[pl.BlockSpec((tm, d), lambda jj, i: (_ahead_row(jj, i), 0)),
                  _ahead_chunk_spec(chunk, tn, n_tiles, n_rows),
                  _ahead_chunk_spec(chunk, tn, n_tiles, n_rows),
                  _side_spec(SIDE_ROWS, width, n_rows, side_valid_blocks)],
        out_specs=[pl.BlockSpec((tm, tn), lambda jj, i: (_ahead_row(jj, i), _ahead_col(jj))),
                   _side_spec(SIDE_ROWS, width, n_rows, side_blocks)],
        out_shape=[jax.ShapeDtypeStruct((m, D_FF_PAD), BF16), jax.ShapeDtypeStruct((D_FF_PAD, width), BF16)],
        scratch_shapes=[pltpu.VMEM((2, d, tn), BF16)] * 2,
        compiler_params=pltpu.CompilerParams(dimension_semantics=("arbitrary", "arbitrary"), vmem_limit_bytes=limit),
        name="ffn_up",
    )(h_bf, w_gate, w_up, w_side)


DOWN_COL_CHUNK = 1024
DOWN_RES_COLS = 512


def _ffn_down_ln_kernel(hid_ref, w_ref, h_ref, g_ref, b_ref, o_ref, *, nk, n_res, last_rows):
    k = pl.program_id(1)

    def accumulate(kk, first):
        hid = hid_ref[:, 0:kk]
        for n0 in range(0, o_ref.shape[1], DOWN_COL_CHUNK):
            part = _dot(hid, w_ref[0:kk, n0:n0 + DOWN_COL_CHUNK])
            if first:
                o_ref[:, n0:n0 + DOWN_COL_CHUNK] = part
            else:
                o_ref[:, n0:n0 + DOWN_COL_CHUNK] += part

    pl.when(k == 0)(functools.partial(accumulate, hid_ref.shape[1], True))
    pl.when(jnp.logical_and(k > 0, k < nk - 1))(functools.partial(accumulate, hid_ref.shape[1], False))
    pl.when(k == nk - 1)(functools.partial(accumulate, last_rows, False))

    @pl.when(k < n_res)
    def _():
        cols = pl.ds(pl.multiple_of(k * DOWN_RES_COLS, DOWN_RES_COLS), DOWN_RES_COLS)
        o_ref[:, cols] += DEEPNORM_ALPHA * h_ref[...]

    @pl.when(k == nk - 1)
    def _():
        _layer_norm_block(o_ref, g_ref, b_ref, (o_ref,))


def _ffn_down_ln(hid, w_down_bf, h, ln_g, ln_b):
    m, d = h.shape
    tm, tk = 1024, 512
    nk = D_FF_PAD // tk
    n_res = d // DOWN_RES_COLS
    assert n_res <= nk
    kern = functools.partial(_ffn_down_ln_kernel, nk=nk, n_res=n_res, last_rows=D_FF - (nk - 1) * tk)
    limit = _vmem_limit([((tm, tk), BF16), ((tk, d), BF16), ((tm, DOWN_RES_COLS), F32), ((tm, d), F32)],
                        [((tm, DOWN_COL_CHUNK), F32)])
    return pl.pallas_call(
        kern,
        grid=(m // tm, nk),
        in_specs=[pl.BlockSpec((tm, tk), lambda i, k: (i, k)),
                  pl.BlockSpec((tk, d), lambda i, k: (k, 0)),
                  pl.BlockSpec((tm, DOWN_RES_COLS), lambda i, k: (i, jnp.minimum(k, n_res - 1))),
                  pl.BlockSpec((1, d), lambda i, k: (0, 0)),
                  pl.BlockSpec((1, d), lambda i, k: (0, 0))],
        out_specs=pl.BlockSpec((tm, d), lambda i, k: (i, 0)),
        out_shape=jax.ShapeDtypeStruct((m, d), F32),
        compiler_params=pltpu.CompilerParams(dimension_semantics=("parallel", "arbitrary"), vmem_limit_bytes=limit),
        name="ffn_down_ln",
    )(hid, w_down_bf, h, ln_g.reshape(1, d), ln_b.reshape(1, d))


def kernel(x, positions, w_in, b_glu, conv_w, conv_b, conv_ln_g, conv_ln_b, lam_q1, lam_k1, lam_q2, lam_k2,
           subln_g, w_o, ln1_g, ln1_b, w_gate, w_up, w_down, ln2_g, ln2_b):
    batch, seq, d = x.shape
    m = batch * seq
    x2 = x.reshape(m, d)
    cos_t, sin_t, x_bf0 = _rope_tables_and_cast(positions, x2)
    for l in range(DEPTH):
        lambda_init = 0.8 - 0.6 * math.exp(-0.3 * l)
        x_bf = x_bf0 if l == 0 else x2.astype(BF16)
        qkv, w_o_bf = _in_proj_qkv(x_bf, w_in[l], cos_t, sin_t, w_o[l])
        glu = _in_proj_glu(x_bf, w_in[l], b_glu[l])
        conv = _conv_ln_swish(glu.reshape(batch, seq, CONV_WIDTH), conv_w[l], conv_b[l], conv_ln_g[l], conv_ln_b[l])
        attn = _diff_attention(qkv, lam_q1[l], lam_k1[l], lam_q2[l], lam_k2[l], subln_g[l], batch, seq, lambda_init)
        h, h_bf = _out_proj_ln(attn, conv.reshape(m, CONV_WIDTH), w_o_bf, x2, ln1_g[l], ln1_b[l])
        hid, w_down_bf = _ffn_up(h_bf, w_gate[l], w_up[l], w_down[l])
        x2 = _ffn_down_ln(hid, w_down_bf, h, ln2_g[l], ln2_b[l])
    return x2.reshape(batch, seq, d)
```
